```python
import math
import jax, jax.numpy as jnp
from jax import lax
import numpy as np

D_MODEL = 1024
BATCH = 32
SEQ = 2048
DEPTH = 4

N_MIXERS = 2
SB_HEADS = 16
SB_HEAD_DIM = D_MODEL // SB_HEADS
Q_BLOCK = 128
GMLP_WIDTH = 2 * D_MODEL
GMLP_GROUPS = 8
GMLP_CHUNK = 128
D_FF = ((8 * D_MODEL // 3 + 127) // 128) * 128
CONV_WIDTH = 3
LN_EPS = 1e-5
DEEPNORM_ALPHA = (2 * DEPTH) ** 0.25
DEEPNORM_BETA = (8 * DEPTH) ** -0.25
N_ATTN_LAYERS = (DEPTH + 1) // 2
N_GMLP_LAYERS = DEPTH // 2

kernel_name = "sb_attn_gmlp_convffn_deepnorm_hybrid"


def layer_norm(h, g, b):
    hf = h.astype(jnp.float32)
    mu = jnp.mean(hf, axis=-1, keepdims=True)
    var = jnp.mean(jnp.square(hf - mu), axis=-1, keepdims=True)
    y = (hf - mu) * lax.rsqrt(var + LN_EPS)
    return (y * g.astype(jnp.float32) + b.astype(jnp.float32)).astype(h.dtype)


def stick_breaking_attention(h, w_in, w_out):
    B, S, _ = h.shape
    qkv = (h @ w_in).reshape(B, S, 3, SB_HEADS, SB_HEAD_DIM)
    q = jnp.transpose(qkv[:, :, 0], (0, 2, 1, 3))
    k = jnp.transpose(qkv[:, :, 1], (0, 2, 1, 3))
    v = jnp.transpose(qkv[:, :, 2], (0, 2, 1, 3))
    scale = SB_HEAD_DIM ** -0.5
    outs = []
    for blk in range(S // Q_BLOCK):
        q0 = blk * Q_BLOCK
        k_end = q0 + Q_BLOCK
        qb = q[:, :, q0:k_end]
        kb = k[:, :, :k_end]
        vb = v[:, :, :k_end]
        z = jnp.einsum('bhqd,bhkd->bhqk', qb, kb).astype(jnp.float32) * scale
        t_idx = q0 + jnp.arange(Q_BLOCK)[:, None]
        s_idx = jnp.arange(k_end)[None, :]
        causal = s_idx < t_idx
        log_beta = jax.nn.log_sigmoid(z)
        log_one_minus = jnp.where(causal, jax.nn.log_sigmoid(-z), 0.0)
        suffix = lax.cumsum(log_one_minus, axis=3, reverse=True) - log_one_minus
        a = jnp.where(causal, jnp.exp(log_beta + suffix), 0.0)
        outs.append(jnp.einsum('bhqk,bhkd->bhqd', a.astype(vb.dtype), vb))
    o = jnp.concatenate(outs, axis=2)
    o = jnp.transpose(o, (0, 2, 1, 3)).reshape(B, S, D_MODEL)
    return o @ w_out


def chunked_spatial_gating(h, w_in, ln_g, ln_b, w_s, b_s, w_out):
    B, S, _ = h.shape
    zz = jax.nn.gelu(h @ w_in)
    u, v = zz[..., :GMLP_WIDTH], zz[..., GMLP_WIDTH:]
    v = layer_norm(v, ln_g, ln_b)
    v = v.reshape(B, S // GMLP_CHUNK, GMLP_CHUNK, GMLP_GROUPS, GMLP_WIDTH // GMLP_GROUPS)
    tri = jnp.tril(jnp.ones((GMLP_CHUNK, GMLP_CHUNK), dtype=bool))
    w_causal = jnp.where(tri[None], w_s, 0.0).astype(v.dtype)
    s = jnp.einsum('gts,bnsgc->bntgc', w_causal, v)
    s = s + jnp.transpose(b_s, (1, 0))[None, None, :, :, None].astype(s.dtype)
    s = s.reshape(B, S, GMLP_WIDTH)
    return (u * s) @ w_out


def causal_depthwise_conv(a, w, b):
    S = a.shape[1]
    pad = CONV_WIDTH - 1
    ap = jnp.pad(a, ((0, 0), (pad, 0), (0, 0)))
    y = b
    for tap in range(CONV_WIDTH):
        y = y + w[tap] * ap[:, tap:tap + S]
    return y


def conv_gated_ffn(h, w_up, conv_w, conv_b, w_down):
    a = h @ w_up
    a = causal_depthwise_conv(a, conv_w, conv_b)
    gate, val = a[..., :D_FF], a[..., D_FF:]
    return (jax.nn.silu(gate) * val) @ w_down


def setup_inputs(seed: int = 0) -> dict:
    key = jax.random.key(seed)
    ks = jax.random.split(key, 20)
    f32 = jnp.float32
    D, E, G, C, F = D_MODEL, GMLP_WIDTH, GMLP_GROUPS, GMLP_CHUNK, D_FF
    nrm = lambda k, shape, std: jax.random.normal(k, shape, f32) * std
    x = nrm(ks[0], (BATCH, SEQ, D), 1.0)
    attn_qk = nrm(ks[1], (N_ATTN_LAYERS, D, 2 * D), D ** -0.5)
    attn_v = nrm(ks[2], (N_ATTN_LAYERS, D, D), D ** -0.5 * DEEPNORM_BETA)
    attn_w_in = jnp.concatenate([attn_qk, attn_v], axis=-1)
    attn_w_out = nrm(ks[3], (N_ATTN_LAYERS, D, D), D ** -0.5 * DEEPNORM_BETA)
    gmlp_w_in = nrm(ks[4], (N_GMLP_LAYERS, D, 2 * E), D ** -0.5)
    gmlp_ln_g = 1.0 + nrm(ks[5], (N_GMLP_LAYERS, E), 0.02)
    gmlp_ln_b = nrm(ks[6], (N_GMLP_LAYERS, E), 0.02)
    gmlp_w_s = nrm(ks[7], (N_GMLP_LAYERS, G, C, C), C ** -0.5)
    gmlp_b_s = 1.0 + nrm(ks[8], (N_GMLP_LAYERS, G, C), 0.02)
    gmlp_w_out = nrm(ks[9], (N_GMLP_LAYERS, E, D), E ** -0.5 * DEEPNORM_BETA)
    ffn_w_up = nrm(ks[10], (DEPTH, D, 2 * F), D ** -0.5)
    ffn_conv_w = nrm(ks[11], (DEPTH, CONV_WIDTH, 2 * F), CONV_WIDTH ** -0.5)
    ffn_conv_b = nrm(ks[12], (DEPTH, 2 * F), 0.01)
    ffn_w_down = nrm(ks[13], (DEPTH, F, D), F ** -0.5 * DEEPNORM_BETA)
    ln_mix_g = 1.0 + nrm(ks[14], (DEPTH, D), 0.02)
    ln_mix_b = nrm(ks[15], (DEPTH, D), 0.02)
    ln_ffn_g = 1.0 + nrm(ks[16], (DEPTH, D), 0.02)
    ln_ffn_b = nrm(ks[17], (DEPTH, D), 0.02)
    return {"x": x, "attn_w_in": attn_w_in, "attn_w_out": attn_w_out,
            "gmlp_w_in": gmlp_w_in, "gmlp_ln_g": gmlp_ln_g, "gmlp_ln_b": gmlp_ln_b,
            "gmlp_w_s": gmlp_w_s, "gmlp_b_s": gmlp_b_s, "gmlp_w_out": gmlp_w_out,
            "ffn_w_up": ffn_w_up, "ffn_conv_w": ffn_conv_w, "ffn_conv_b": ffn_conv_b,
            "ffn_w_down": ffn_w_down, "ln_mix_g": ln_mix_g, "ln_mix_b": ln_mix_b,
            "ln_ffn_g": ln_ffn_g, "ln_ffn_b": ln_ffn_b}


def reference(x, attn_w_in, attn_w_out, gmlp_w_in, gmlp_ln_g, gmlp_ln_b, gmlp_w_s, gmlp_b_s,
              gmlp_w_out, ffn_w_up, ffn_conv_w, ffn_conv_b, ffn_w_down,
              ln_mix_g, ln_mix_b, ln_ffn_g, ln_ffn_b):
    h = x
    for i in range(DEPTH):
        j = i // N_MIXERS
        if i % N_MIXERS == 0:
            m = stick_breaking_attention(h, attn_w_in[j], attn_w_out[j])
        else:
            m = chunked_spatial_gating(h, gmlp_w_in[j], gmlp_ln_g[j], gmlp_ln_b[j],
                                       gmlp_w_s[j], gmlp_b_s[j], gmlp_w_out[j])
        h = layer_norm(DEEPNORM_ALPHA * h + m, ln_mix_g[i], ln_mix_b[i])
        f = conv_gated_ffn(h, ffn_w_up[i], ffn_conv_w[i], ffn_conv_b[i], ffn_w_down[i])
        h = layer_norm(DEEPNORM_ALPHA * h + f, ln_ffn_g[i], ln_ffn_b[i])
    return h
```

```python
import functools
import math

import jax
import jax.numpy as jnp
from jax import lax
from jax.experimental import pallas as pl
from jax.experimental.pallas import tpu as pltpu

LN_EPS = 1e-5
CONV_WIDTH = 3
SB_HEAD_DIM = 64
GMLP_GROUPS = 8
GMLP_CHUNK = 128

SUBLANES = 8
LANES = 128
KEY_BLOCK = 128
KEY_GROUP = KEY_BLOCK // SUBLANES
VMEM_LIMIT_BYTES = 56 * 1024 * 1024

_BF16 = jnp.bfloat16
_F32 = jnp.float32


def _resident(block_shape, index_map):
    return pl.BlockSpec(block_shape, index_map, pipeline_mode=pl.Buffered(1))


def _layer_norm(y, g, b):
    mu = jnp.mean(y, axis=-1, keepdims=True)
    d = y - mu
    var = jnp.mean(d * d, axis=-1, keepdims=True)
    return d * lax.rsqrt(var + LN_EPS) * g + b


def _dot(a, b):
    return jnp.dot(a, b, preferred_element_type=_F32)


def _dot_nt(a, b):
    return lax.dot_general(a, b, (((1,), (1,)), ((), ())), preferred_element_type=_F32)


def _qkv_kernel(h_ref, wq_ref, wk_ref, wvt_ref, q_ref, kp_ref, vt_ref, *, tm, scale):
    hb = h_ref[0].astype(_BF16)
    row = lax.broadcasted_iota(jnp.int32, (KEY_BLOCK, KEY_BLOCK), 0)
    col = lax.broadcasted_iota(jnp.int32, (KEY_BLOCK, KEY_BLOCK), 1)
    perm = (col == (row % SUBLANES) * KEY_GROUP + row // SUBLANES).astype(_BF16)
    hp = jnp.concatenate(
        [_dot(perm, hb[blk * KEY_BLOCK:(blk + 1) * KEY_BLOCK, :]) for blk in range(tm // KEY_BLOCK)],
        axis=0).astype(_BF16)
    q_ref[0] = (_dot(hb, wq_ref[...]) * scale).astype(_BF16)
    kp_ref[0] = _dot(hp, wk_ref[...]).astype(_BF16)
    vt_ref[0] = _dot_nt(wvt_ref[...], hp).astype(_BF16)


def _qkv_proj(h, wq, wk, wvt):
    B, S, D = h.shape
    tm = min(512, S)
    scale = SB_HEAD_DIM ** -0.5
    return pl.pallas_call(
        functools.partial(_qkv_kernel, tm=tm, scale=scale),
        grid=(B, S // tm),
        in_specs=[
            pl.BlockSpec((1, tm, D), lambda b, s: (b, s, 0)),
            _resident((D, D), lambda b, s: (0, 0)),
            _resident((D, D), lambda b, s: (0, 0)),
            _resident((D, D), lambda b, s: (0, 0)),
        ],
        out_specs=[
            pl.BlockSpec((1, tm, D), lambda b, s: (b, s, 0)),
            pl.BlockSpec((1, tm, D), lambda b, s: (b, s, 0)),
            pl.BlockSpec((1, D, tm), lambda b, s: (b, 0, s)),
        ],
        out_shape=[
            jax.ShapeDtypeStruct((B, S, D), _BF16),
            jax.ShapeDtypeStruct((B, S, D), _BF16),
            jax.ShapeDtypeStruct((B, D, S), _BF16),
        ],
        compiler_params=pltpu.CompilerParams(
            dimension_semantics=("parallel", "parallel"), vmem_limit_bytes=VMEM_LIMIT_BYTES),
        name="qkv_proj",
    )(h, wq, wk, wvt)


def _sublane_suffix_scan(g):
    row = lax.broadcasted_iota(jnp.int32, g.shape, 0)
    x = g
    for sh in (1, 2, 4):
        shifted = pltpu.roll(x, SUBLANES - sh, axis=0)
        x = x + jnp.where(row + sh < SUBLANES, shifted, 0.0)
    return x


def _sb_block(z, carry, causal):
    sp = jnp.log1p(jnp.exp(-jnp.abs(z)))
    log_beta = jnp.minimum(z, 0.0) - sp
    lom = -jnp.maximum(z, 0.0) - sp
    if causal is not None:
        lom = jnp.where(causal, lom, 0.0)
    tiles = [lom[SUBLANES * v:SUBLANES * (v + 1), :] for v in range(KEY_GROUP)]
    run = [None] * KEY_GROUP
    run[KEY_GROUP - 1] = jnp.zeros_like(tiles[0])
    for v in range(KEY_GROUP - 2, -1, -1):
        run[v] = run[v + 1] + tiles[v + 1]
    group_tot = run[0] + tiles[0]
    incl = _sublane_suffix_scan(group_tot)
    base = (incl - group_tot) + carry
    suffix = jnp.concatenate([run[v] + base for v in range(KEY_GROUP)], axis=0)
    a = jnp.exp(log_beta + suffix)
    if causal is not None:
        a = jnp.where(causal, a, 0.0)
    new_carry = carry + jnp.broadcast_to(incl[0:1, :], carry.shape)
    return a.astype(_BF16), new_carry


def _attn_kernel(q_ref, kp_ref, vt_ref, o_ref, *, tq):
    qi = pl.program_id(2)
    q = q_ref[0]
    lane = lax.broadcasted_iota(jnp.int32, q.shape, 1)
    qm = [jnp.where((lane // SB_HEAD_DIM) == hh, q, jnp.zeros_like(q)) for hh in range(2)]

    row = lax.broadcasted_iota(jnp.int32, (KEY_BLOCK, tq), 0)
    key_off = (row % SUBLANES) * KEY_GROUP + row // SUBLANES
    causal = key_off < lax.broadcasted_iota(jnp.int32, (KEY_BLOCK, tq), 1)

    def block(j, state, mask):
        start = pl.multiple_of(j * KEY_BLOCK, KEY_BLOCK)
        kblk = kp_ref[0, pl.ds(start, KEY_BLOCK), :]
        out = []
        for hh in range(2):
            acc, carry = state[hh]
            z = _dot_nt(kblk, qm[hh])
            a, carry = _sb_block(z, carry, mask)
            vt = vt_ref[0, SB_HEAD_DIM * hh:SB_HEAD_DIM * (hh + 1), pl.ds(start, KEY_BLOCK)]
            out.append((acc + _dot(vt, a), carry))
        return tuple(out)

    zero = (jnp.zeros((SB_HEAD_DIM, tq), _F32), jnp.zeros((SUBLANES, tq), _F32))
    state = block(qi, (zero, zero), causal)
    state = lax.fori_loop(0, qi, lambda i, st: block(qi - 1 - i, st, None), state)
    o_t = jnp.concatenate([state[0][0], state[1][0]], axis=0)
    o_ref[0] = o_t.T.astype(o_ref.dtype)


def _sb_attention(q, kp, vt):
    B, S, D = q.shape
    tq = KEY_BLOCK
    hp = D // LANES
    return pl.pallas_call(
        functools.partial(_attn_kernel, tq=tq),
        grid=(B, hp, S // tq),
        in_specs=[
            pl.BlockSpec((1, tq, LANES), lambda b, h, i: (b, i, h)),
            pl.BlockSpec((1, S, LANES), lambda b, h, i: (b, 0, h)),
            pl.BlockSpec((1, LANES, S), lambda b, h, i: (b, h, 0)),
        ],
        out_specs=pl.BlockSpec((1, tq, LANES), lambda b, h, i: (b, i, h)),
        out_shape=jax.ShapeDtypeStruct((B, S, D), _BF16),
        compiler_params=pltpu.CompilerParams(
            dimension_semantics=("parallel", "parallel", "parallel"),
            vmem_limit_bytes=VMEM_LIMIT_BYTES),
        name="sb_attention",
    )(q, kp, vt)


def _mix_out_kernel(o_ref, h_ref, w_ref, g_ref, b_ref, out_ref, *, alpha):
    m = _dot(o_ref[0], w_ref[...])
    out_ref[0] = _layer_norm(alpha * h_ref[0] + m, g_ref[...], b_ref[...])


def _mix_out_ln(o, h, w_out, g, b, alpha):
    B, S, D = h.shape
    tm = min(512, S)
    return pl.pallas_call(
        functools.partial(_mix_out_kernel, alpha=alpha),
        grid=(B, S // tm),
        in_specs=[
            pl.BlockSpec((1, tm, D), lambda b, s: (b, s, 0)),
            pl.BlockSpec((1, tm, D), lambda b, s: (b, s, 0)),
            _resident((D, D), lambda b, s: (0, 0)),
            _resident((1, D), lambda b, s: (0, 0)),
            _resident((1, D), lambda b, s: (0, 0)),
        ],
        out_specs=pl.BlockSpec((1, tm, D), lambda b, s: (b, s, 0)),
        out_shape=jax.ShapeDtypeStruct((B, S, D), _F32),
        compiler_params=pltpu.CompilerParams(
            dimension_semantics=("parallel", "parallel"), vmem_limit_bytes=VMEM_LIMIT_BYTES),
        name="mix_out_ln",
    )(o, h, w_out, g, b)


def _gelu_tanh(x):
    c = math.sqrt(2.0 / math.pi)
    return 0.5 * x * (1.0 + jnp.tanh(c * (x + 0.044715 * (x * x * x))))


def _gmlp_kernel(x_ref, win_ref, lng_ref, lnb_ref, ws_ref, bs_ref, wout_ref, g_ref, b_ref,
                 out_ref, *, tm, width, alpha):
    x = x_ref[0]
    xb = x.astype(_BF16)
    u = _gelu_tanh(_dot(xb, win_ref[:, :width]))
    v = _gelu_tanh(_dot(xb, win_ref[:, width:]))
    vn = _layer_norm(v, lng_ref[...], lnb_ref[...]).astype(_BF16)
    gw = width // GMLP_GROUPS
    tri = (lax.broadcasted_iota(jnp.int32, (GMLP_CHUNK, GMLP_CHUNK), 0)
           >= lax.broadcasted_iota(jnp.int32, (GMLP_CHUNK, GMLP_CHUNK), 1))
    w_causal = [jnp.where(tri, ws_ref[g], 0.0).astype(_BF16) for g in range(GMLP_GROUPS)]
    chunks = []
    for ch in range(tm // GMLP_CHUNK):
        r0 = ch * GMLP_CHUNK
        cols = []
        for g in range(GMLP_GROUPS):
            cols.append(_dot(w_causal[g], vn[r0:r0 + GMLP_CHUNK, g * gw:(g + 1) * gw]))
        chunks.append(jnp.concatenate(cols, axis=1) + bs_ref[...])
    s = jnp.concatenate(chunks, axis=0)
    m = _dot((u * s).astype(_BF16), wout_ref[...])
    out_ref[0] = _layer_norm(alpha * x + m, g_ref[...], b_ref[...])


def _gmlp_mixer(h, w_in, ln_g, ln_b, w_s, bs_full, w_out, g, b, alpha):
    B, S, D = h.shape
    width = w_out.shape[0]
    tm = min(256, S)
    const2 = lambda b, s: (0, 0)
    return pl.pallas_call(
        functools.partial(_gmlp_kernel, tm=tm, width=width, alpha=alpha),
        grid=(B, S // tm),
        in_specs=[
            pl.BlockSpec((1, tm, D), lambda b, s: (b, s, 0)),
            _resident((D, 2 * width), const2),
            _resident((1, width), const2),
            _resident((1, width), const2),
            _resident((GMLP_GROUPS, GMLP_CHUNK, GMLP_CHUNK), lambda b, s: (0, 0, 0)),
            _resident((GMLP_CHUNK, width), const2),
            _resident((width, D), const2),
            _resident((1, D), const2),
            _resident((1, D), const2),
        ],
        out_specs=pl.BlockSpec((1, tm, D), lambda b, s: (b, s, 0)),
        out_shape=jax.ShapeDtypeStruct((B, S, D), _F32),
        compiler_params=pltpu.CompilerParams(
            dimension_semantics=("parallel", "parallel"), vmem_limit_bytes=VMEM_LIMIT_BYTES),
        name="gmlp_mixer",
    )(h, w_in, ln_g, ln_b, w_s, bs_full, w_out, g, b)


def _ffn_kernel(x_ref, wup_ref, cw_ref, cb_ref, wdown_ref, g_ref, b_ref, out_ref,
                acc_ref, tail_ref, *, tm, n_chunks, alpha):
    first = pl.program_id(1) == 0
    x = x_ref[0]
    xb = x.astype(_BF16)
    acc_ref[...] = jnp.zeros_like(acc_ref)

    @pl.when(first)
    def _():
        tail_ref[...] = jnp.zeros_like(tail_ref)

    def conv_half(idx):
        a = _dot(xb, wup_ref[idx])
        prev = tail_ref[idx]
        tail_ref[idx] = a[tm - SUBLANES:, :]
        ext = jnp.concatenate([prev, a], axis=0)
        a1 = pltpu.roll(ext, 1, axis=0)[SUBLANES:, :]
        a2 = pltpu.roll(ext, 2, axis=0)[SUBLANES:, :]
        w = cw_ref[idx]
        return cb_ref[idx] + w[0:1, :] * a2 + w[1:2, :] * a1 + w[2:3, :] * a

    def chunk(c, carry):
        gate = conv_half(c)
        val = conv_half(n_chunks + c)
        gated = (gate * jax.nn.sigmoid(gate) * val).astype(_BF16)
        acc_ref[...] += _dot(gated, wdown_ref[c])
        return carry

    lax.fori_loop(0, n_chunks, chunk, 0)
    out_ref[0] = _layer_norm(alpha * x + acc_ref[...], g_ref[...], b_ref[...])


def _conv_ffn(h, wup_c, cw_c, cb_c, wdown_c, g, b, alpha):
    B, S, D = h.shape
    n2, _, fc = wup_c.shape
    n_chunks = n2 // 2
    tm = min(512, S)
    const2 = lambda b, s: (0, 0)
    const3 = lambda b, s: (0, 0, 0)
    return pl.pallas_call(
        functools.partial(_ffn_kernel, tm=tm, n_chunks=n_chunks, alpha=alpha),
        grid=(B, S // tm),
        in_specs=[
            pl.BlockSpec((1, tm, D), lambda b, s: (b, s, 0)),
            _resident((n2, D, fc), const3),
            _resident((n2, CONV_WIDTH, fc), const3),
            _resident((n2, 1, fc), const3),
            _resident((n_chunks, fc, D), const3),
            _resident((1, D), const2),
            _resident((1, D), const2),
        ],
        out_specs=pl.BlockSpec((1, tm, D), lambda b, s: (b, s, 0)),
        out_shape=jax.ShapeDtypeStruct((B, S, D), _F32),
        scratch_shapes=[
            pltpu.VMEM((tm, D), _F32),
            pltpu.VMEM((n2, SUBLANES, fc), _F32),
        ],
        compiler_params=pltpu.CompilerParams(
            dimension_semantics=("parallel", "arbitrary"), vmem_limit_bytes=VMEM_LIMIT_BYTES),
        name="conv_ffn",
    )(h, wup_c, cw_c, cb_c, wdown_c, g, b)


FFN_CHUNK = 256


def _chunk_cols(w, fc):
    r, c = w.shape
    return jnp.transpose(w.reshape(r, c // fc, fc), (1, 0, 2))


def kernel(x, attn_w_in, attn_w_out, gmlp_w_in, gmlp_ln_g, gmlp_ln_b, gmlp_w_s, gmlp_b_s, gmlp_w_out,
           ffn_w_up, ffn_conv_w, ffn_conv_b, ffn_w_down, ln_mix_g, ln_mix_b, ln_ffn_g, ln_ffn_b):
    B, S, D = x.shape
    depth = ffn_w_up.shape[0]
    alpha = (2 * depth) ** 0.25
    d_ff = ffn_w_down.shape[1]
    width = gmlp_w_out.shape[1]
    row = lambda v: v.reshape(1, -1)

    h = x
    for i in range(depth):
        j = i // 2
        if i % 2 == 0:
            w_in = attn_w_in[j]
            wq = w_in[:, :D].astype(_BF16)
            wk = w_in[:, D:2 * D].astype(_BF16)
            wvt = w_in[:, 2 * D:].T.astype(_BF16)
            q, kp, vt = _qkv_proj(h, wq, wk, wvt)
            o = _sb_attention(q, kp, vt)
            h = _mix_out_ln(o, h, attn_w_out[j].astype(_BF16), row(ln_mix_g[i]), row(ln_mix_b[i]), alpha)
        else:
            bs_full = jnp.repeat(gmlp_b_s[j].T, width // GMLP_GROUPS, axis=1)
            h = _gmlp_mixer(h, gmlp_w_in[j].astype(_BF16), row(gmlp_ln_g[j]), row(gmlp_ln_b[j]),
                            gmlp_w_s[j], bs_full, gmlp_w_out[j].astype(_BF16),
                            row(ln_mix_g[i]), row(ln_mix_b[i]), alpha)
        wup_c = _chunk_cols(ffn_w_up[i].astype(_BF16), FFN_CHUNK)
        cw_c = _chunk_cols(ffn_conv_w[i], FFN_CHUNK)
        cb_c = _chunk_cols(ffn_conv_b[i].reshape(1, -1), FFN_CHUNK)
        wdown_c = ffn_w_down[i].astype(_BF16).reshape(d_ff // FFN_CHUNK, FFN_CHUNK, D)
        h = _conv_ffn(h, wup_c, cw_c, cb_c, wdown_c, row(ln_ffn_g[i]), row(ln_ffn_b[i]), alpha)
    return h
```

```python
import functools
import math

import jax
import jax.numpy as jnp
from jax import lax
from jax.experimental import pallas as pl
from jax.experimental.pallas import tpu as pltpu

LN_EPS = 1e-5
CONV_WIDTH = 3
SB_HEAD_DIM = 64
GMLP_GROUPS = 8
GMLP_CHUNK = 128

SUBLANES = 8
LANES = 128
KEY_BLOCK = 128
KEY_GROUP = KEY_BLOCK // SUBLANES
VMEM_LIMIT_BYTES = 56 * 1024 * 1024

ATT_TQ = 256
ATT_KB = 256
ATT_LANES = 256
LOG2E = 1.0 / math.log(2.0)
INV_LN2 = 1.0 / math.log(2.0)
LOG2_DEAD = -150.0

_BF16 = jnp.bfloat16
_F32 = jnp.float32


def _resident(block_shape, index_map):
    return pl.BlockSpec(block_shape, index_map, pipeline_mode=pl.Buffered(1))


def _layer_norm(y, g, b):
    mu = jnp.mean(y, axis=-1, keepdims=True)
    d = y - mu
    var = jnp.mean(d * d, axis=-1, keepdims=True)
    return d * lax.rsqrt(var + LN_EPS) * g + b


def _dot(a, b):
    return jnp.dot(a, b, preferred_element_type=_F32)


def _dot_nt(a, b):
    return lax.dot_general(a, b, (((1,), (1,)), ((), ())), preferred_element_type=_F32)


def _qkv_kernel(h_ref, wq_ref, wk_ref, wvt_ref, q_ref, kp_ref, vt_ref, *, tm, scale):
    hb = h_ref[0].astype(_BF16)
    row = lax.broadcasted_iota(jnp.int32, (KEY_BLOCK, KEY_BLOCK), 0)
    col = lax.broadcasted_iota(jnp.int32, (KEY_BLOCK, KEY_BLOCK), 1)
    perm = (col == (row % SUBLANES) * KEY_GROUP + row // SUBLANES).astype(_BF16)
    hp = jnp.concatenate(
        [_dot(perm, hb[blk * KEY_BLOCK:(blk + 1) * KEY_BLOCK, :]) for blk in range(tm // KEY_BLOCK)],
        axis=0).astype(_BF16)
    q_ref[0] = (_dot(hb, wq_ref[...]) * scale).astype(_BF16)
    kp_ref[0] = _dot(hp, wk_ref[...]).astype(_BF16)
    vt_ref[0] = _dot_nt(wvt_ref[...], hp).astype(_BF16)


def _qkv_proj(h, wq, wk, wvt):
    B, S, D = h.shape
    tm = min(512, S)
    scale = SB_HEAD_DIM ** -0.5 * LOG2E
    return pl.pallas_call(
        functools.partial(_qkv_kernel, tm=tm, scale=scale),
        grid=(B, S // tm),
        in_specs=[
            pl.BlockSpec((1, tm, D), lambda b, s: (b, s, 0)),
            _resident((D, D), lambda b, s: (0, 0)),
            _resident((D, D), lambda b, s: (0, 0)),
            _resident((D, D), lambda b, s: (0, 0)),
        ],
        out_specs=[
            pl.BlockSpec((1, tm, D), lambda b, s: (b, s, 0)),
            pl.BlockSpec((1, tm, D), lambda b, s: (b, s, 0)),
            pl.BlockSpec((1, D, tm), lambda b, s: (b, 0, s)),
        ],
        out_shape=[
            jax.ShapeDtypeStruct((B, S, D), _BF16),
            jax.ShapeDtypeStruct((B, S, D), _BF16),
            jax.ShapeDtypeStruct((B, D, S), _BF16),
        ],
        compiler_params=pltpu.CompilerParams(
            dimension_semantics=("parallel", "parallel"), vmem_limit_bytes=VMEM_LIMIT_BYTES),
        name="qkv_proj",
    )(h, wq, wk, wvt)


def _sublane_suffix_scan(g):
    row = lax.broadcasted_iota(jnp.int32, g.shape, 0)
    x = g
    for sh in (1, 2, 4):
        shifted = pltpu.roll(x, SUBLANES - sh, axis=0)
        x = x + jnp.where(row + sh < SUBLANES, shifted, 0.0)
    return x


def _sb_block(z, carry, causal):
    sp = jnp.log(1.0 + jnp.exp2(-jnp.abs(z))) * INV_LN2
    log_beta = jnp.minimum(z, 0.0) - sp
    lom = log_beta - z
    if causal is not None:
        lom = jnp.where(causal, lom, 0.0)
    tiles = [lom[SUBLANES * v:SUBLANES * (v + 1), :] for v in range(KEY_GROUP)]
    run = [None] * KEY_GROUP
    run[KEY_GROUP - 1] = jnp.zeros_like(tiles[0])
    for v in range(KEY_GROUP - 2, -1, -1):
        run[v] = run[v + 1] + tiles[v + 1]
    group_tot = run[0] + tiles[0]
    incl = _sublane_suffix_scan(group_tot)
    base = (incl - group_tot) + carry
    suffix = jnp.concatenate([run[v] + base for v in range(KEY_GROUP)], axis=0)
    a = jnp.exp2(log_beta + suffix)
    if causal is not None:
        a = jnp.where(causal, a, 0.0)
    new_carry = carry + jnp.broadcast_to(incl[0:1, :], carry.shape)
    return a.astype(_BF16), new_carry


def _attn_kernel(q_ref, kp_ref, vt_ref, o_ref, acc_ref, carry_ref, *, tq, n_heads):
    qi = pl.program_id(2)
    lane = lax.broadcasted_iota(jnp.int32, (tq, LANES), 1)
    qm = []
    for h in range(n_heads):
        qpair = q_ref[0, :, LANES * (h // 2):LANES * (h // 2 + 1)]
        qm.append(jnp.where((lane // SB_HEAD_DIM) == h % 2, qpair, jnp.zeros_like(qpair)))

    acc_ref[...] = jnp.zeros_like(acc_ref)
    carry_ref[...] = jnp.zeros_like(carry_ref)

    row = lax.broadcasted_iota(jnp.int32, (KEY_BLOCK, tq), 0)
    key_off = (row % SUBLANES) * KEY_GROUP + row // SUBLANES
    query_off = lax.broadcasted_iota(jnp.int32, (KEY_BLOCK, tq), 1)

    def key_step(j, diagonal):
        n_sub = ATT_KB // KEY_BLOCK
        for h in range(n_heads):
            a_parts = [None] * n_sub
            for sub in range(n_sub - 1, -1, -1):
                start = pl.multiple_of(j * ATT_KB + sub * KEY_BLOCK, KEY_BLOCK)
                kblk = kp_ref[0, pl.ds(start, KEY_BLOCK), LANES * (h // 2):LANES * (h // 2 + 1)]
                z = _dot_nt(kblk, qm[h])
                mask = (key_off + sub * KEY_BLOCK < query_off) if diagonal else None
                a_parts[sub], carry_ref[h] = _sb_block(z, carry_ref[h], mask)
            vt = vt_ref[0, SB_HEAD_DIM * h:SB_HEAD_DIM * (h + 1),
                        pl.ds(pl.multiple_of(j * ATT_KB, ATT_KB), ATT_KB)]
            acc_ref[SB_HEAD_DIM * h:SB_HEAD_DIM * (h + 1), :] += _dot(vt, jnp.concatenate(a_parts, axis=0))

    def any_alive():
        return jnp.max(carry_ref[...]) > LOG2_DEAD

    key_step(qi, True)

    def body(state):
        j, _ = state
        key_step(j, False)
        return j - 1, any_alive()

    lax.while_loop(lambda st: jnp.logical_and(st[0] >= 0, st[1]), body, (qi - 1, any_alive()))
    o_ref[0] = acc_ref[...].T.astype(o_ref.dtype)


def _sb_attention(q, kp, vt):
    B, S, D = q.shape
    tq = min(ATT_TQ, S)
    assert tq == ATT_KB and S % tq == 0 and D % ATT_LANES == 0
    n_heads = ATT_LANES // SB_HEAD_DIM
    return pl.pallas_call(
        functools.partial(_attn_kernel, tq=tq, n_heads=n_heads),
        grid=(B, D // ATT_LANES, S // tq),
        in_specs=[
            pl.BlockSpec((1, tq, ATT_LANES), lambda b, h, i: (b, i, h)),
            pl.BlockSpec((1, S, ATT_LANES), lambda b, h, i: (b, 0, h)),
            pl.BlockSpec((1, ATT_LANES, S), lambda b, h, i: (b, h, 0)),
        ],
        out_specs=pl.BlockSpec((1, tq, ATT_LANES), lambda b, h, i: (b, i, h)),
        out_shape=jax.ShapeDtypeStruct((B, S, D), _BF16),
        scratch_shapes=[
            pltpu.VMEM((ATT_LANES, tq), _F32),
            pltpu.VMEM((n_heads, SUBLANES, tq), _F32),
        ],
        compiler_params=pltpu.CompilerParams(
            dimension_semantics=("parallel", "parallel", "parallel"),
            vmem_limit_bytes=VMEM_LIMIT_BYTES),
        name="sb_attention",
    )(q, kp, vt)


def _mix_out_kernel(o_ref, h_ref, w_ref, g_ref, b_ref, out_ref, *, alpha):
    m = _dot(o_ref[0], w_ref[...])
    out_ref[0] = _layer_norm(alpha * h_ref[0] + m, g_ref[...], b_ref[...])


def _mix_out_ln(o, h, w_out, g, b, alpha):
    B, S, D = h.shape
    tm = min(512, S)
    return pl.pallas_call(
        functools.partial(_mix_out_kernel, alpha=alpha),
        grid=(B, S // tm),
        in_specs=[
            pl.BlockSpec((1, tm, D), lambda b, s: (b, s, 0)),
            pl.BlockSpec((1, tm, D), lambda b, s: (b, s, 0)),
            _resident((D, D), lambda b, s: (0, 0)),
            _resident((1, D), lambda b, s: (0, 0)),
            _resident((1, D), lambda b, s: (0, 0)),
        ],
        out_specs=pl.BlockSpec((1, tm, D), lambda b, s: (b, s, 0)),
        out_shape=jax.ShapeDtypeStruct((B, S, D), _F32),
        compiler_params=pltpu.CompilerParams(
            dimension_semantics=("parallel", "parallel"), vmem_limit_bytes=VMEM_LIMIT_BYTES),
        name="mix_out_ln",
    )(o, h, w_out, g, b)


def _gelu_tanh(x):
    c = math.sqrt(2.0 / math.pi)
    return 0.5 * x * (1.0 + jnp.tanh(c * (x + 0.044715 * (x * x * x))))


def _gmlp_kernel(x_ref, win_ref, lng_ref, lnb_ref, ws_ref, bs_ref, wout_ref, g_ref, b_ref,
                 out_ref, *, tm, width, alpha):
    x = x_ref[0]
    xb = x.astype(_BF16)
    u = _gelu_tanh(_dot(xb, win_ref[:, :width]))
    v = _gelu_tanh(_dot(xb, win_ref[:, width:]))
    vn = _layer_norm(v, lng_ref[...], lnb_ref[...]).astype(_BF16)
    gw = width // GMLP_GROUPS
    tri = (lax.broadcasted_iota(jnp.int32, (GMLP_CHUNK, GMLP_CHUNK), 0)
           >= lax.broadcasted_iota(jnp.int32, (GMLP_CHUNK, GMLP_CHUNK), 1))
    w_causal = [jnp.where(tri, ws_ref[g], 0.0).astype(_BF16) for g in range(GMLP_GROUPS)]
    chunks = []
    for ch in range(tm // GMLP_CHUNK):
        r0 = ch * GMLP_CHUNK
        cols = []
        for g in range(GMLP_GROUPS):
            cols.append(_dot(w_causal[g], vn[r0:r0 + GMLP_CHUNK, g * gw:(g + 1) * gw]))
        chunks.append(jnp.concatenate(cols, axis=1) + bs_ref[...])
    s = jnp.concatenate(chunks, axis=0)
    m = _dot((u * s).astype(_BF16), wout_ref[...])
    out_ref[0] = _layer_norm(alpha * x + m, g_ref[...], b_ref[...])


def _gmlp_mixer(h, w_in, ln_g, ln_b, w_s, bs_full, w_out, g, b, alpha):
    B, S, D = h.shape
    width = w_out.shape[0]
    tm = min(256, S)
    const2 = lambda b, s: (0, 0)
    return pl.pallas_call(
        functools.partial(_gmlp_kernel, tm=tm, width=width, alpha=alpha),
        grid=(B, S // tm),
        in_specs=[
            pl.BlockSpec((1, tm, D), lambda b, s: (b, s, 0)),
            _resident((D, 2 * width), const2),
            _resident((1, width), const2),
            _resident((1, width), const2),
            _resident((GMLP_GROUPS, GMLP_CHUNK, GMLP_CHUNK), lambda b, s: (0, 0, 0)),
            _resident((GMLP_CHUNK, width), const2),
            _resident((width, D), const2),
            _resident((1, D), const2),
            _resident((1, D), const2),
        ],
        out_specs=pl.BlockSpec((1, tm, D), lambda b, s: (b, s, 0)),
        out_shape=jax.ShapeDtypeStruct((B, S, D), _F32),
        compiler_params=pltpu.CompilerParams(
            dimension_semantics=("parallel", "parallel"), vmem_limit_bytes=VMEM_LIMIT_BYTES),
        name="gmlp_mixer",
    )(h, w_in, ln_g, ln_b, w_s, bs_full, w_out, g, b)


def _ffn_kernel(x_ref, wup_ref, cw_ref, cb_ref, wdown_ref, g_ref, b_ref, out_ref,
                acc_ref, tail_ref, *, tm, n_chunks, alpha):
    first = pl.program_id(1) == 0
    x = x_ref[0]
    xb = x.astype(_BF16)
    acc_ref[...] = jnp.zeros_like(acc_ref)

    @pl.when(first)
    def _():
        tail_ref[...] = jnp.zeros_like(tail_ref)

    def conv_half(idx):
        a = _dot(xb, wup_ref[idx])
        prev = tail_ref[idx]
        tail_ref[idx] = a[tm - SUBLANES:, :]
        ext = jnp.concatenate([prev, a], axis=0)
        a1 = pltpu.roll(ext, 1, axis=0)[SUBLANES:, :]
        a2 = pltpu.roll(ext, 2, axis=0)[SUBLANES:, :]
        w = cw_ref[idx]
        return cb_ref[idx] + w[0:1, :] * a2 + w[1:2, :] * a1 + w[2:3, :] * a

    def chunk(c, carry):
        gate = conv_half(c)
        val = conv_half(n_chunks + c)
        gated = (gate * jax.nn.sigmoid(gate) * val).astype(_BF16)
        acc_ref[...] += _dot(gated, wdown_ref[c])
        return carry

    lax.fori_loop(0, n_chunks, chunk, 0)
    out_ref[0] = _layer_norm(alpha * x + acc_ref[...], g_ref[...], b_ref[...])


def _conv_ffn(h, wup_c, cw_c, cb_c, wdown_c, g, b, alpha):
    B, S, D = h.shape
    n2, _, fc = wup_c.shape
    n_chunks = n2 // 2
    tm = min(512, S)
    const2 = lambda b, s: (0, 0)
    const3 = lambda b, s: (0, 0, 0)
    return pl.pallas_call(
        functools.partial(_ffn_kernel, tm=tm, n_chunks=n_chunks, alpha=alpha),
        grid=(B, S // tm),
        in_specs=[
            pl.BlockSpec((1, tm, D), lambda b, s: (b, s, 0)),
            _resident((n2, D, fc), const3),
            _resident((n2, CONV_WIDTH, fc), const3),
            _resident((n2, 1, fc), const3),
            _resident((n_chunks, fc, D), const3),
            _resident((1, D), const2),
            _resident((1, D), const2),
        ],
        out_specs=pl.BlockSpec((1, tm, D), lambda b, s: (b, s, 0)),
        out_shape=jax.ShapeDtypeStruct((B, S, D), _F32),
        scratch_shapes=[
            pltpu.VMEM((tm, D), _F32),
            pltpu.VMEM((n2, SUBLANES, fc), _F32),
        ],
        compiler_params=pltpu.CompilerParams(
            dimension_semantics=("parallel", "arbitrary"), vmem_limit_bytes=VMEM_LIMIT_BYTES),
        name="conv_ffn",
    )(h, wup_c, cw_c, cb_c, wdown_c, g, b)


FFN_CHUNK = 256


def _chunk_cols(w, fc):
    r, c = w.shape
    return jnp.transpose(w.reshape(r, c // fc, fc), (1, 0, 2))


def kernel(x, attn_w_in, attn_w_out, gmlp_w_in, gmlp_ln_g, gmlp_ln_b, gmlp_w_s, gmlp_b_s, gmlp_w_out,
           ffn_w_up, ffn_conv_w, ffn_conv_b, ffn_w_down, ln_mix_g, ln_mix_b, ln_ffn_g, ln_ffn_b):
    B, S, D = x.shape
    depth = ffn_w_up.shape[0]
    alpha = (2 * depth) ** 0.25
    d_ff = ffn_w_down.shape[1]
    width = gmlp_w_out.shape[1]
    row = lambda v: v.reshape(1, -1)

    h = x
    for i in range(depth):
        j = i // 2
        if i % 2 == 0:
            w_in = attn_w_in[j]
            wq = w_in[:, :D].astype(_BF16)
            wk = w_in[:, D:2 * D].astype(_BF16)
            wvt = w_in[:, 2 * D:].T.astype(_BF16)
            q, kp, vt = _qkv_proj(h, wq, wk, wvt)
            o = _sb_attention(q, kp, vt)
            h = _mix_out_ln(o, h, attn_w_out[j].astype(_BF16), row(ln_mix_g[i]), row(ln_mix_b[i]), alpha)
        else:
            bs_full = jnp.repeat(gmlp_b_s[j].T, width // GMLP_GROUPS, axis=1)
            h = _gmlp_mixer(h, gmlp_w_in[j].astype(_BF16), row(gmlp_ln_g[j]), row(gmlp_ln_b[j]),
                            gmlp_w_s[j], bs_full, gmlp_w_out[j].astype(_BF16),
                            row(ln_mix_g[i]), row(ln_mix_b[i]), alpha)
        wup_c = _chunk_cols(ffn_w_up[i].astype(_BF16), FFN_CHUNK)
        cw_c = _chunk_cols(ffn_conv_w[i], FFN_CHUNK)
        cb_c = _chunk_cols(ffn_conv_b[i].reshape(1, -1), FFN_CHUNK)
        wdown_c = ffn_w_down[i].astype(_BF16).reshape(d_ff // FFN_CHUNK, FFN_CHUNK, D)
        h = _conv_ffn(h, wup_c, cw_c, cb_c, wdown_c, row(ln_ffn_g[i]), row(ln_ffn_b[i]), alpha)
    return h
```

```python
import functools
import math

import jax
import jax.numpy as jnp
from jax import lax
from jax.experimental import pallas as pl
from jax.experimental.pallas import tpu as pltpu

LN_EPS = 1e-5
CONV_WIDTH = 3
SB_HEAD_DIM = 64
GMLP_GROUPS = 8
GMLP_CHUNK = 128

SUBLANES = 8
LANES = 128
MXU_TILE = 256
KEY_BLOCK = 128
KEY_GROUP = KEY_BLOCK // SUBLANES
VMEM_LIMIT_BYTES = 56 * 1024 * 1024

ATT_TQ = 256
ATT_KB = 256
ATT_LANES = 512
ATT_CHAIN = 3
LOG2E = 1.0 / math.log(2.0)
INV_LN2 = 1.0 / math.log(2.0)
LOG2_DEAD = -150.0

_BF16 = jnp.bfloat16
_F32 = jnp.float32


def _resident(block_shape, index_map):
    return pl.BlockSpec(block_shape, index_map, pipeline_mode=pl.Buffered(1))


def _layer_norm(y, g, b):
    mu = jnp.mean(y, axis=-1, keepdims=True)
    d = y - mu
    var = jnp.mean(d * d, axis=-1, keepdims=True)
    return d * lax.rsqrt(var + LN_EPS) * g + b


def _dot(a, b):
    return jnp.dot(a, b, preferred_element_type=_F32)


def _dot_nt(a, b):
    return lax.dot_general(a, b, (((1,), (1,)), ((), ())), preferred_element_type=_F32)


def _qkv_kernel(h_ref, wq_ref, wk_ref, wvt_ref, q_ref, kp_ref, vt_ref, *, tm, scale):
    hb = h_ref[0].astype(_BF16)
    row = lax.broadcasted_iota(jnp.int32, (KEY_BLOCK, KEY_BLOCK), 0)
    col = lax.broadcasted_iota(jnp.int32, (KEY_BLOCK, KEY_BLOCK), 1)
    perm = (col == (row % SUBLANES) * KEY_GROUP + row // SUBLANES).astype(_BF16)
    hp = jnp.concatenate(
        [_dot(perm, hb[blk * KEY_BLOCK:(blk + 1) * KEY_BLOCK, :]) for blk in range(tm // KEY_BLOCK)],
        axis=0).astype(_BF16)
    q_ref[0] = (_dot(hb, wq_ref[...]) * scale).astype(_BF16)
    kp_ref[0] = _dot(hp, wk_ref[...]).astype(_BF16)
    vt_ref[0] = _dot_nt(wvt_ref[...], hp).astype(_BF16)


def _qkv_proj(h, wq, wk, wvt):
    B, S, D = h.shape
    tm = min(512, S)
    scale = SB_HEAD_DIM ** -0.5 * LOG2E
    return pl.pallas_call(
        functools.partial(_qkv_kernel, tm=tm, scale=scale),
        grid=(B, S // tm),
        in_specs=[
            pl.BlockSpec((1, tm, D), lambda b, s: (b, s, 0)),
            _resident((D, D), lambda b, s: (0, 0)),
            _resident((D, D), lambda b, s: (0, 0)),
            _resident((D, D), lambda b, s: (0, 0)),
        ],
        out_specs=[
            pl.BlockSpec((1, tm, D), lambda b, s: (b, s, 0)),
            pl.BlockSpec((1, tm, D), lambda b, s: (b, s, 0)),
            pl.BlockSpec((1, D, tm), lambda b, s: (b, 0, s)),
        ],
        out_shape=[
            jax.ShapeDtypeStruct((B, S, D), _BF16),
            jax.ShapeDtypeStruct((B, S, D), _BF16),
            jax.ShapeDtypeStruct((B, D, S), _BF16),
        ],
        compiler_params=pltpu.CompilerParams(
            dimension_semantics=("parallel", "parallel"), vmem_limit_bytes=VMEM_LIMIT_BYTES),
        name="qkv_proj",
    )(h, wq, wk, wvt)


def _sublane_suffix_scan(g):
    row = lax.broadcasted_iota(jnp.int32, g.shape, 0)
    x = g
    for sh in (1, 2, 4):
        shifted = pltpu.roll(x, SUBLANES - sh, axis=0)
        x = x + jnp.where(row + sh < SUBLANES, shifted, 0.0)
    return x


def _sb_block(z, carry, causal):
    sp = jnp.log(1.0 + jnp.exp2(-jnp.abs(z))) * INV_LN2
    log_beta = jnp.minimum(z, 0.0) - sp
    lom = log_beta - z
    if causal is not None:
        lom = jnp.where(causal, lom, 0.0)
    tiles = [lom[SUBLANES * v:SUBLANES * (v + 1), :] for v in range(KEY_GROUP)]
    run = [None] * KEY_GROUP
    run[KEY_GROUP - 1] = jnp.zeros_like(tiles[0])
    for v in range(KEY_GROUP - 2, -1, -1):
        run[v] = run[v + 1] + tiles[v + 1]
    group_tot = run[0] + tiles[0]
    incl = _sublane_suffix_scan(group_tot)
    base = (incl - group_tot) + carry
    suffix = jnp.concatenate([run[v] + base for v in range(KEY_GROUP)], axis=0)
    a = jnp.exp2(log_beta + suffix)
    if causal is not None:
        a = jnp.where(causal, a, 0.0)
    new_carry = carry + jnp.broadcast_to(incl[0:1, :], carry.shape)
    return a.astype(_BF16), new_carry


def _attn_kernel(q_ref, kp_ref, vt_ref, o_ref, acc_ref, carry_ref, *, tq, n_heads):
    qi = pl.program_id(2)
    lane = lax.broadcasted_iota(jnp.int32, (tq, LANES), 1)
    qm = []
    for h in range(n_heads):
        qpair = q_ref[0, :, LANES * (h // 2):LANES * (h // 2 + 1)]
        qm.append(jnp.where((lane // SB_HEAD_DIM) == h % 2, qpair, jnp.zeros_like(qpair)))

    acc_ref[...] = jnp.zeros_like(acc_ref)
    carry_ref[...] = jnp.zeros_like(carry_ref)

    row = lax.broadcasted_iota(jnp.int32, (KEY_BLOCK, tq), 0)
    key_off = (row % SUBLANES) * KEY_GROUP + row // SUBLANES
    query_off = lax.broadcasted_iota(jnp.int32, (KEY_BLOCK, tq), 1)

    def key_step(j, diagonal):
        n_sub = ATT_KB // KEY_BLOCK

        def scores(h):
            out = [None] * n_sub
            for sub in range(n_sub - 1, -1, -1):
                start = pl.multiple_of(j * ATT_KB + sub * KEY_BLOCK, KEY_BLOCK)
                kblk = kp_ref[0, pl.ds(start, KEY_BLOCK), LANES * (h // 2):LANES * (h // 2 + 1)]
                out[sub] = _dot_nt(kblk, qm[h])
            return out

        z = {h: scores(h) for h in range(min(ATT_CHAIN, n_heads))}
        for h in range(n_heads):
            a_parts = [None] * n_sub
            for sub in range(n_sub - 1, -1, -1):
                mask = (key_off + sub * KEY_BLOCK < query_off) if diagonal else None
                a_parts[sub], carry_ref[h] = _sb_block(z[h][sub], carry_ref[h], mask)
            if h + ATT_CHAIN < n_heads:
                z[h + ATT_CHAIN] = scores(h + ATT_CHAIN)
            vt = vt_ref[0, SB_HEAD_DIM * h:SB_HEAD_DIM * (h + 1),
                        pl.ds(pl.multiple_of(j * ATT_KB, ATT_KB), ATT_KB)]
            acc_ref[SB_HEAD_DIM * h:SB_HEAD_DIM * (h + 1), :] += _dot(vt, jnp.concatenate(a_parts, axis=0))

    def any_alive():
        return jnp.max(carry_ref[...]) > LOG2_DEAD

    key_step(qi, True)

    def body(state):
        j, _ = state
        key_step(j, False)
        return j - 1, any_alive()

    lax.while_loop(lambda st: jnp.logical_and(st[0] >= 0, st[1]), body, (qi - 1, any_alive()))
    o_ref[0] = acc_ref[...].T.astype(o_ref.dtype)


def _sb_attention(q, kp, vt):
    B, S, D = q.shape
    tq = min(ATT_TQ, S)
    assert tq == ATT_KB and S % tq == 0 and D % ATT_LANES == 0
    n_heads = ATT_LANES // SB_HEAD_DIM
    return pl.pallas_call(
        functools.partial(_attn_kernel, tq=tq, n_heads=n_heads),
        grid=(B, D // ATT_LANES, S // tq),
        in_specs=[
            pl.BlockSpec((1, tq, ATT_LANES), lambda b, h, i: (b, i, h)),
            pl.BlockSpec((1, S, ATT_LANES), lambda b, h, i: (b, 0, h)),
            pl.BlockSpec((1, ATT_LANES, S), lambda b, h, i: (b, h, 0)),
        ],
        out_specs=pl.BlockSpec((1, tq, ATT_LANES), lambda b, h, i: (b, i, h)),
        out_shape=jax.ShapeDtypeStruct((B, S, D), _BF16),
        scratch_shapes=[
            pltpu.VMEM((ATT_LANES, tq), _F32),
            pltpu.VMEM((n_heads, SUBLANES, tq), _F32),
        ],
        compiler_params=pltpu.CompilerParams(
            dimension_semantics=("parallel", "parallel", "parallel"),
            vmem_limit_bytes=VMEM_LIMIT_BYTES),
        name="sb_attention",
    )(q, kp, vt)


def _mix_out_kernel(o_ref, h_ref, w_ref, g_ref, b_ref, out_ref, *, alpha):
    m = _dot(o_ref[0], w_ref[...])
    out_ref[0] = _layer_norm(alpha * h_ref[0] + m, g_ref[...], b_ref[...])


def _mix_out_ln(o, h, w_out, g, b, alpha):
    B, S, D = h.shape
    tm = min(512, S)
    return pl.pallas_call(
        functools.partial(_mix_out_kernel, alpha=alpha),
        grid=(B, S // tm),
        in_specs=[
            pl.BlockSpec((1, tm, D), lambda b, s: (b, s, 0)),
            pl.BlockSpec((1, tm, D), lambda b, s: (b, s, 0)),
            _resident((D, D), lambda b, s: (0, 0)),
            _resident((1, D), lambda b, s: (0, 0)),
            _resident((1, D), lambda b, s: (0, 0)),
        ],
        out_specs=pl.BlockSpec((1, tm, D), lambda b, s: (b, s, 0)),
        out_shape=jax.ShapeDtypeStruct((B, S, D), _F32),
        compiler_params=pltpu.CompilerParams(
            dimension_semantics=("parallel", "parallel"), vmem_limit_bytes=VMEM_LIMIT_BYTES),
        name="mix_out_ln",
    )(o, h, w_out, g, b)


def _gelu_tanh(x):
    c = math.sqrt(2.0 / math.pi)
    return 0.5 * x * (1.0 + jnp.tanh(c * (x + 0.044715 * (x * x * x))))


def _gmlp_kernel(x_ref, win_ref, lng_ref, lnb_ref, ws_ref, bs_ref, wout_ref, g_ref, b_ref,
                 out_ref, *, tm, width, alpha):
    x = x_ref[0]
    xb = x.astype(_BF16)
    u = _gelu_tanh(_dot(xb, win_ref[:, :width]))
    v = _gelu_tanh(_dot(xb, win_ref[:, width:]))
    vn = _layer_norm(v, lng_ref[...], lnb_ref[...]).astype(_BF16)
    gw = width // GMLP_GROUPS
    tri = (lax.broadcasted_iota(jnp.int32, (GMLP_CHUNK, GMLP_CHUNK), 0)
           >= lax.broadcasted_iota(jnp.int32, (GMLP_CHUNK, GMLP_CHUNK), 1))
    w_causal = [jnp.where(tri, ws_ref[g], 0.0).astype(_BF16) for g in range(GMLP_GROUPS)]
    chunks = []
    for ch in range(tm // GMLP_CHUNK):
        r0 = ch * GMLP_CHUNK
        cols = []
        for g in range(GMLP_GROUPS):
            cols.append(_dot(w_causal[g], vn[r0:r0 + GMLP_CHUNK, g * gw:(g + 1) * gw]))
        chunks.append(jnp.concatenate(cols, axis=1) + bs_ref[...])
    s = jnp.concatenate(chunks, axis=0)
    m = _dot((u * s).astype(_BF16), wout_ref[...])
    out_ref[0] = _layer_norm(alpha * x + m, g_ref[...], b_ref[...])


def _gmlp_mixer(h, w_in, ln_g, ln_b, w_s, bs_full, w_out, g, b, alpha):
    B, S, D = h.shape
    width = w_out.shape[0]
    tm = min(256, S)
    const2 = lambda b, s: (0, 0)
    return pl.pallas_call(
        functools.partial(_gmlp_kernel, tm=tm, width=width, alpha=alpha),
        grid=(B, S // tm),
        in_specs=[
            pl.BlockSpec((1, tm, D), lambda b, s: (b, s, 0)),
            _resident((D, 2 * width), const2),
            _resident((1, width), const2),
            _resident((1, width), const2),
            _resident((GMLP_GROUPS, GMLP_CHUNK, GMLP_CHUNK), lambda b, s: (0, 0, 0)),
            _resident((GMLP_CHUNK, width), const2),
            _resident((width, D), const2),
            _resident((1, D), const2),
            _resident((1, D), const2),
        ],
        out_specs=pl.BlockSpec((1, tm, D), lambda b, s: (b, s, 0)),
        out_shape=jax.ShapeDtypeStruct((B, S, D), _F32),
        compiler_params=pltpu.CompilerParams(
            dimension_semantics=("parallel", "parallel"), vmem_limit_bytes=VMEM_LIMIT_BYTES),
        name="gmlp_mixer",
    )(h, w_in, ln_g, ln_b, w_s, bs_full, w_out, g, b)


PACKED_ROWS = 2 * SUBLANES


def _slab_tokens(v, zero):
    u = pltpu.bitcast(v, jnp.uint32)
    tokens = []
    for j in range(u.shape[0] // SUBLANES):
        rows = u[SUBLANES * j:SUBLANES * (j + 1), :]
        t = rows[:, 0:LANES]
        for l in range(1, u.shape[1] // LANES):
            t = t | rows[:, LANES * l:LANES * (l + 1)]
        tokens.append(t & zero)
    return tokens


def _tie_to_tokens(x, tokens, k_tile):
    u = pltpu.bitcast(x, jnp.uint32)
    n_slabs = u.shape[0] // SUBLANES
    n_lanes = u.shape[1] // LANES
    pieces = [[u[SUBLANES * i:SUBLANES * (i + 1), LANES * l:LANES * (l + 1)] for l in range(n_lanes)]
              for i in range(n_slabs)]
    n_pos = (u.shape[1] // k_tile) * n_slabs
    for j, tok in enumerate(tokens):
        p = j * n_pos // len(tokens)
        k, i = p // n_slabs, p % n_slabs
        l = k * (k_tile // LANES)
        pieces[i][l] = pieces[i][l] | tok
    u = jnp.concatenate([jnp.concatenate(row, axis=1) for row in pieces], axis=0)
    return pltpu.bitcast(u, _BF16)


def _ffn_kernel(x_ref, wup_ref, cw_ref, cb_ref, wdown_ref, g_ref, b_ref, zero_ref, out_ref,
                acc_ref, tail_ref, xb_ref, a0_ref, a1_ref, a2_ref, a3_ref, gated0_ref, gated1_ref,
                *, tm, rs, n_chunks, alpha):
    a_refs = (a0_ref, a1_ref, a2_ref, a3_ref)
    gated_refs = (gated0_ref, gated1_ref)
    n_slabs = tm // rs
    n_items = n_chunks * n_slabs
    xb_ref[...] = x_ref[0].astype(_BF16)
    acc_ref[...] = jnp.zeros_like(acc_ref)

    @pl.when(pl.program_id(1) == 0)
    def _():
        tail_ref[...] = jnp.zeros_like(tail_ref)

    def item(t):
        return t // n_slabs, pl.multiple_of((t % n_slabs) * rs, rs)

    def up(t, slot, tokens=()):
        c, r0 = item(t)
        xs = xb_ref[pl.ds(r0, rs), :]
        if tokens:
            xs = _tie_to_tokens(xs, tokens, MXU_TILE)
        for half in range(2):
            a_refs[slot][half] = _dot(xs, wup_ref[half * n_chunks + c])

    def conv_gate(t, slot_a, slot_g):
        c, _ = item(t)

        def conv_half(half):
            idx = half * n_chunks + c
            a = a_refs[slot_a][half]
            prev = tail_ref[idx]
            tail_ref[idx] = a[rs - SUBLANES:, :]
            ext = jnp.concatenate([prev, a], axis=0)
            a1 = pltpu.roll(ext, 1, axis=0)[SUBLANES:, :]
            a2 = pltpu.roll(ext, 2, axis=0)[SUBLANES:, :]
            w = cw_ref[idx]
            return cb_ref[idx] + w[0:1, :] * a2 + w[1:2, :] * a1 + w[2:3, :] * a

        gate = conv_half(0)
        val = conv_half(1)
        gated = (gate * jax.nn.sigmoid(gate) * val).astype(_BF16)
        gated_refs[slot_g][...] = gated
        return _slab_tokens(gated, zero_ref[...])

    def down(t, slot):
        c, r0 = item(t)
        acc_ref[pl.ds(r0, rs), :] += _dot(gated_refs[slot][...], wdown_ref[c])

    def stage(t, k, do_down=True, do_up=True):
        tokens = conv_gate(t, k % 4, k % 2)
        if do_down:
            down(t - 1, (k - 1) % 2)
        if do_up:
            up(t + 2, (k + 2) % 4, tokens)

    up(0, 0)
    up(1, 1)
    stage(0, 0, do_down=False)
    stage(1, 1)

    def four(q, carry):
        t = 4 * q + 2
        for k in range(4):
            stage(t + k, (2 + k) % 4)
        return carry

    lax.fori_loop(0, (n_items - 4) // 4, four, 0)
    stage(n_items - 2, (n_items - 2) % 4, do_up=False)
    stage(n_items - 1, (n_items - 1) % 4, do_up=False)
    down(n_items - 1, (n_items - 1) % 2)
    out_ref[0] = _layer_norm(alpha * x_ref[0] + acc_ref[...], g_ref[...], b_ref[...])


def _conv_ffn(h, wup_c, cw_c, cb_c, wdown_c, g, b, alpha):
    B, S, D = h.shape
    n2, _, fc = wup_c.shape
    n_chunks = n2 // 2
    tm = min(FFN_ROWS, S)
    rs = min(FFN_SLAB, tm)
    n_items = n_chunks * (tm // rs)
    assert n_items >= 8 and n_items % 4 == 0
    const2 = lambda b, s: (0, 0)
    const3 = lambda b, s: (0, 0, 0)
    return pl.pallas_call(
        functools.partial(_ffn_kernel, tm=tm, rs=rs, n_chunks=n_chunks, alpha=alpha),
        grid=(B, S // tm),
        in_specs=[
            pl.BlockSpec((1, tm, D), lambda b, s: (b, s, 0)),
            _resident((n2, D, fc), const3),
            _resident((n2, CONV_WIDTH, fc), const3),
            _resident((n2, 1, fc), const3),
            _resident((n_chunks, fc, D), const3),
            _resident((1, D), const2),
            _resident((1, D), const2),
            _resident((SUBLANES, LANES), const2),
        ],
        out_specs=pl.BlockSpec((1, tm, D), lambda b, s: (b, s, 0)),
        out_shape=jax.ShapeDtypeStruct((B, S, D), _F32),
        scratch_shapes=[
            pltpu.VMEM((tm, D), _F32),
            pltpu.VMEM((n2, SUBLANES, fc), _F32),
            pltpu.VMEM((tm, D), _BF16),
            pltpu.VMEM((2, rs, fc), _F32),
            pltpu.VMEM((2, rs, fc), _F32),
            pltpu.VMEM((2, rs, fc), _F32),
            pltpu.VMEM((2, rs, fc), _F32),
            pltpu.VMEM((rs, fc), _BF16),
            pltpu.VMEM((rs, fc), _BF16),
        ],
        compiler_params=pltpu.CompilerParams(
            dimension_semantics=("parallel", "arbitrary"), vmem_limit_bytes=VMEM_LIMIT_BYTES),
        name="conv_ffn",
    )(h, wup_c, cw_c, cb_c, wdown_c, g, b, jnp.zeros((SUBLANES, LANES), jnp.uint32))


FFN_ROWS = 1024
FFN_SLAB = 256
FFN_CHUNK = 256


def _chunk_cols(w, fc):
    r, c = w.shape
    return jnp.transpose(w.reshape(r, c // fc, fc), (1, 0, 2))


def kernel(x, attn_w_in, attn_w_out, gmlp_w_in, gmlp_ln_g, gmlp_ln_b, gmlp_w_s, gmlp_b_s, gmlp_w_out,
           ffn_w_up, ffn_conv_w, ffn_conv_b, ffn_w_down, ln_mix_g, ln_mix_b, ln_ffn_g, ln_ffn_b):
    B, S, D = x.shape
    depth = ffn_w_up.shape[0]
    alpha = (2 * depth) ** 0.25
    d_ff = ffn_w_down.shape[1]
    width = gmlp_w_out.shape[1]
    row = lambda v: v.reshape(1, -1)

    h = x
    for i in range(depth):
        j = i // 2
        if i % 2 == 0:
            w_in = attn_w_in[j]
            wq = w_in[:, :D].astype(_BF16)
            wk = w_in[:, D:2 * D].astype(_BF16)
            wvt = w_in[:, 2 * D:].T.astype(_BF16)
            q, kp, vt = _qkv_proj(h, wq, wk, wvt)
            o = _sb_attention(q, kp, vt)
            h = _mix_out_ln(o, h, attn_w_out[j].astype(_BF16), row(ln_mix_g[i]), row(ln_mix_b[i]), alpha)
        else:
            bs_full = jnp.repeat(gmlp_b_s[j].T, width // GMLP_GROUPS, axis=1)
            h = _gmlp_mixer(h, gmlp_w_in[j].astype(_BF16), row(gmlp_ln_g[j]), row(gmlp_ln_b[j]),
                            gmlp_w_s[j], bs_full, gmlp_w_out[j].astype(_BF16),
                            row(ln_mix_g[i]), row(ln_mix_b[i]), alpha)
        wup_c = _chunk_cols(ffn_w_up[i].astype(_BF16), FFN_CHUNK)
        cw_c = _chunk_cols(ffn_conv_w[i], FFN_CHUNK)
        cb_c = _chunk_cols(ffn_conv_b[i].reshape(1, -1), FFN_CHUNK)
        wdown_c = ffn_w_down[i].astype(_BF16).reshape(d_ff // FFN_CHUNK, FFN_CHUNK, D)
        h = _conv_ffn(h, wup_c, cw_c, cb_c, wdown_c, row(ln_ffn_g[i]), row(ln_ffn_b[i]), alpha)
    return h
```

```python
import functools
import math

import jax
import jax.numpy as jnp
from jax import lax
from jax.experimental import pallas as pl
from jax.experimental.pallas import tpu as pltpu

LN_EPS = 1e-5
CONV_WIDTH = 3
SB_HEAD_DIM = 64
GMLP_GROUPS = 8
GMLP_CHUNK = 128

SUBLANES = 8
LANES = 128
MXU_TILE = 256
KEY_BLOCK = 128
KEY_GROUP = KEY_BLOCK // SUBLANES
VMEM_LIMIT_BYTES = 56 * 1024 * 1024

ATT_TQ = 256
ATT_KB = 256
ATT_LANES = 512
ATT_CHAIN = 3
LOG2E = 1.0 / math.log(2.0)
INV_LN2 = 1.0 / math.log(2.0)
LOG2_DEAD = 150.0
SIGN_BIT = 0x80000000

_BF16 = jnp.bfloat16
_F32 = jnp.float32


def _resident(block_shape, index_map):
    return pl.BlockSpec(block_shape, index_map, pipeline_mode=pl.Buffered(1))


def _layer_norm(y, g, b):
    mu = jnp.mean(y, axis=-1, keepdims=True)
    d = y - mu
    var = jnp.mean(d * d, axis=-1, keepdims=True)
    return d * lax.rsqrt(var + LN_EPS) * g + b


def _dot(a, b):
    return jnp.dot(a, b, preferred_element_type=_F32)


def _dot_nt(a, b):
    return lax.dot_general(a, b, (((1,), (1,)), ((), ())), preferred_element_type=_F32)


def _qkv_kernel(h_ref, wq_ref, wk_ref, wvt_ref, q_ref, kp_ref, vt_ref, *, tm, scale):
    hb = h_ref[0].astype(_BF16)
    row = lax.broadcasted_iota(jnp.int32, (KEY_BLOCK, KEY_BLOCK), 0)
    col = lax.broadcasted_iota(jnp.int32, (KEY_BLOCK, KEY_BLOCK), 1)
    perm = (col == (row % SUBLANES) * KEY_GROUP + row // SUBLANES).astype(_BF16)
    hp = jnp.concatenate(
        [_dot(perm, hb[blk * KEY_BLOCK:(blk + 1) * KEY_BLOCK, :]) for blk in range(tm // KEY_BLOCK)],
        axis=0).astype(_BF16)
    q_ref[0] = (_dot(hb, wq_ref[...]) * scale).astype(_BF16)
    kp_ref[0] = _dot(hp, wk_ref[...]).astype(_BF16)
    vt_ref[0] = _dot_nt(wvt_ref[...], hp).astype(_BF16)


def _qkv_proj(h, wq, wk, wvt):
    B, S, D = h.shape
    tm = min(512, S)
    scale = SB_HEAD_DIM ** -0.5 * LOG2E
    return pl.pallas_call(
        functools.partial(_qkv_kernel, tm=tm, scale=scale),
        grid=(B, S // tm),
        in_specs=[
            pl.BlockSpec((1, tm, D), lambda b, s: (b, s, 0)),
            _resident((D, D), lambda b, s: (0, 0)),
            _resident((D, D), lambda b, s: (0, 0)),
            _resident((D, D), lambda b, s: (0, 0)),
        ],
        out_specs=[
            pl.BlockSpec((1, tm, D), lambda b, s: (b, s, 0)),
            pl.BlockSpec((1, tm, D), lambda b, s: (b, s, 0)),
            pl.BlockSpec((1, D, tm), lambda b, s: (b, 0, s)),
        ],
        out_shape=[
            jax.ShapeDtypeStruct((B, S, D), _BF16),
            jax.ShapeDtypeStruct((B, S, D), _BF16),
            jax.ShapeDtypeStruct((B, D, S), _BF16),
        ],
        compiler_params=pltpu.CompilerParams(
            dimension_semantics=("parallel", "parallel"), vmem_limit_bytes=VMEM_LIMIT_BYTES),
        name="qkv_proj",
    )(h, wq, wk, wvt)


def _sublane_suffix_scan(g):
    row = lax.broadcasted_iota(jnp.int32, g.shape, 0)
    x = g
    for sh in (1, 2, 4):
        shifted = pltpu.roll(x, SUBLANES - sh, axis=0)
        x = x + jnp.where(row + sh < SUBLANES, shifted, 0.0)
    return x


def _sb_block(z, carry, causal):
    neg_abs = pltpu.bitcast(pltpu.bitcast(z, jnp.uint32) | jnp.uint32(SIGN_BIT), _F32)
    m = jnp.maximum(z, 0.0) + jnp.log(1.0 + jnp.exp2(neg_abs)) * INV_LN2
    if causal is not None:
        m = jnp.where(causal, m, 0.0)
    tiles = [m[SUBLANES * v:SUBLANES * (v + 1), :] for v in range(KEY_GROUP)]
    run = [None] * KEY_GROUP
    run[KEY_GROUP - 1] = tiles[KEY_GROUP - 1]
    for v in range(KEY_GROUP - 2, -1, -1):
        run[v] = run[v + 1] + tiles[v]
    group_tot = run[0]
    incl = _sublane_suffix_scan(group_tot)
    base = (incl - group_tot) + carry
    total = jnp.concatenate([run[v] + base for v in range(KEY_GROUP)], axis=0)
    a = jnp.exp2(z - total)
    if causal is not None:
        a = jnp.where(causal, a, 0.0)
    new_carry = carry + jnp.broadcast_to(incl[0:1, :], carry.shape)
    return a.astype(_BF16), new_carry


def _attn_kernel(q_ref, kp_ref, vt_ref, o_ref, acc_ref, carry_ref, *, tq, n_heads):
    qi = pl.program_id(2)
    lane = lax.broadcasted_iota(jnp.int32, (tq, LANES), 1)
    qm = []
    for h in range(n_heads):
        qpair = q_ref[0, :, LANES * (h // 2):LANES * (h // 2 + 1)]
        qm.append(jnp.where((lane // SB_HEAD_DIM) == h % 2, qpair, jnp.zeros_like(qpair)))

    acc_ref[...] = jnp.zeros_like(acc_ref)
    carry_ref[...] = jnp.zeros_like(carry_ref)

    row = lax.broadcasted_iota(jnp.int32, (KEY_BLOCK, tq), 0)
    key_off = (row % SUBLANES) * KEY_GROUP + row // SUBLANES
    query_off = lax.broadcasted_iota(jnp.int32, (KEY_BLOCK, tq), 1)

    def key_step(j, diagonal):
        n_sub = ATT_KB // KEY_BLOCK

        def scores(h):
            out = [None] * n_sub
            for sub in range(n_sub - 1, -1, -1):
                start = pl.multiple_of(j * ATT_KB + sub * KEY_BLOCK, KEY_BLOCK)
                kblk = kp_ref[0, pl.ds(start, KEY_BLOCK), LANES * (h // 2):LANES * (h // 2 + 1)]
                out[sub] = _dot_nt(kblk, qm[h])
            return out

        z = {h: scores(h) for h in range(min(ATT_CHAIN, n_heads))}
        for h in range(n_heads):
            a_parts = [None] * n_sub
            for sub in range(n_sub - 1, -1, -1):
                mask = (key_off + sub * KEY_BLOCK < query_off) if diagonal else None
                a_parts[sub], carry_ref[h] = _sb_block(z[h][sub], carry_ref[h], mask)
            if h + ATT_CHAIN < n_heads:
                z[h + ATT_CHAIN] = scores(h + ATT_CHAIN)
            vt = vt_ref[0, SB_HEAD_DIM * h:SB_HEAD_DIM * (h + 1),
                        pl.ds(pl.multiple_of(j * ATT_KB, ATT_KB), ATT_KB)]
            acc_ref[SB_HEAD_DIM * h:SB_HEAD_DIM * (h + 1), :] += _dot(vt, jnp.concatenate(a_parts, axis=0))

    def any_alive():
        return jnp.min(carry_ref[...]) < LOG2_DEAD

    key_step(qi, True)

    def body(state):
        j, _ = state
        key_step(j, False)
        return j - 1, any_alive()

    lax.while_loop(lambda st: jnp.logical_and(st[0] >= 0, st[1]), body, (qi - 1, any_alive()))
    o_ref[0] = acc_ref[...].T.astype(o_ref.dtype)


def _sb_attention(q, kp, vt):
    B, S, D = q.shape
    tq = min(ATT_TQ, S)
    assert tq == ATT_KB and S % tq == 0 and D % ATT_LANES == 0
    n_heads = ATT_LANES // SB_HEAD_DIM
    return pl.pallas_call(
        functools.partial(_attn_kernel, tq=tq, n_heads=n_heads),
        grid=(B, D // ATT_LANES, S // tq),
        in_specs=[
            pl.BlockSpec((1, tq, ATT_LANES), lambda b, h, i: (b, i, h)),
            pl.BlockSpec((1, S, ATT_LANES), lambda b, h, i: (b, 0, h)),
            pl.BlockSpec((1, ATT_LANES, S), lambda b, h, i: (b, h, 0)),
        ],
        out_specs=pl.BlockSpec((1, tq, ATT_LANES), lambda b, h, i: (b, i, h)),
        out_shape=jax.ShapeDtypeStruct((B, S, D), _BF16),
        scratch_shapes=[
            pltpu.VMEM((ATT_LANES, tq), _F32),
            pltpu.VMEM((n_heads, SUBLANES, tq), _F32),
        ],
        compiler_params=pltpu.CompilerParams(
            dimension_semantics=("parallel", "parallel", "parallel"),
            vmem_limit_bytes=VMEM_LIMIT_BYTES),
        name="sb_attention",
    )(q, kp, vt)


def _mix_out_kernel(o_ref, h_ref, w_ref, g_ref, b_ref, out_ref, *, alpha):
    m = _dot(o_ref[0], w_ref[...])
    out_ref[0] = _layer_norm(alpha * h_ref[0] + m, g_ref[...], b_ref[...])


def _mix_out_ln(o, h, w_out, g, b, alpha):
    B, S, D = h.shape
    tm = min(512, S)
    return pl.pallas_call(
        functools.partial(_mix_out_kernel, alpha=alpha),
        grid=(B, S // tm),
        in_specs=[
            pl.BlockSpec((1, tm, D), lambda b, s: (b, s, 0)),
            pl.BlockSpec((1, tm, D), lambda b, s: (b, s, 0)),
            _resident((D, D), lambda b, s: (0, 0)),
            _resident((1, D), lambda b, s: (0, 0)),
            _resident((1, D), lambda b, s: (0, 0)),
        ],
        out_specs=pl.BlockSpec((1, tm, D), lambda b, s: (b, s, 0)),
        out_shape=jax.ShapeDtypeStruct((B, S, D), _F32),
        compiler_params=pltpu.CompilerParams(
            dimension_semantics=("parallel", "parallel"), vmem_limit_bytes=VMEM_LIMIT_BYTES),
        name="mix_out_ln",
    )(o, h, w_out, g, b)


def _gelu_tanh(x):
    c = math.sqrt(2.0 / math.pi)
    return 0.5 * x * (1.0 + jnp.tanh(c * (x + 0.044715 * (x * x * x))))


def _gmlp_kernel(x_ref, win_ref, lng_ref, lnb_ref, ws_ref, bs_ref, wout_ref, g_ref, b_ref,
                 out_ref, *, tm, rt, width, alpha):
    gw = width // GMLP_GROUPS
    tri = (lax.broadcasted_iota(jnp.int32, (GMLP_CHUNK, GMLP_CHUNK), 0)
           >= lax.broadcasted_iota(jnp.int32, (GMLP_CHUNK, GMLP_CHUNK), 1))
    w_causal = [jnp.where(tri, ws_ref[g], 0.0).astype(_BF16) for g in range(GMLP_GROUPS)]

    def proj_in(i):
        xb = x_ref[0, i * rt:(i + 1) * rt, :].astype(_BF16)
        return _dot(xb, win_ref[:, :width]), _dot(xb, win_ref[:, width:])

    zz = proj_in(0)
    for i in range(tm // rt):
        nxt = proj_in(i + 1) if (i + 1) * rt < tm else None
        u = _gelu_tanh(zz[0])
        v = _gelu_tanh(zz[1])
        vn = _layer_norm(v, lng_ref[...], lnb_ref[...]).astype(_BF16)
        chunks = []
        for ch in range(rt // GMLP_CHUNK):
            r0 = ch * GMLP_CHUNK
            cols = []
            for g in range(GMLP_GROUPS):
                cols.append(_dot(w_causal[g], vn[r0:r0 + GMLP_CHUNK, g * gw:(g + 1) * gw]))
            chunks.append(jnp.concatenate(cols, axis=1) + bs_ref[...])
        s = jnp.concatenate(chunks, axis=0)
        m = _dot((u * s).astype(_BF16), wout_ref[...])
        rows = slice(i * rt, (i + 1) * rt)
        out_ref[0, rows, :] = _layer_norm(alpha * x_ref[0, rows, :] + m, g_ref[...], b_ref[...])
        zz = nxt


def _gmlp_mixer(h, w_in, ln_g, ln_b, w_s, bs_full, w_out, g, b, alpha):
    B, S, D = h.shape
    width = w_out.shape[0]
    tm = min(GMLP_ROWS, S)
    rt = min(GMLP_SUBTILE, tm)
    const2 = lambda b, s: (0, 0)
    return pl.pallas_call(
        functools.partial(_gmlp_kernel, tm=tm, rt=rt, width=width, alpha=alpha),
        grid=(B, S // tm),
        in_specs=[
            pl.BlockSpec((1, tm, D), lambda b, s: (b, s, 0)),
            _resident((D, 2 * width), const2),
            _resident((1, width), const2),
            _resident((1, width), const2),
            _resident((GMLP_GROUPS, GMLP_CHUNK, GMLP_CHUNK), lambda b, s: (0, 0, 0)),
            _resident((GMLP_CHUNK, width), const2),
            _resident((width, D), const2),
            _resident((1, D), const2),
            _resident((1, D), const2),
        ],
        out_specs=pl.BlockSpec((1, tm, D), lambda b, s: (b, s, 0)),
        out_shape=jax.ShapeDtypeStruct((B, S, D), _F32),
        compiler_params=pltpu.CompilerParams(
            dimension_semantics=("parallel", "parallel"), vmem_limit_bytes=VMEM_LIMIT_BYTES),
        name="gmlp_mixer",
    )(h, w_in, ln_g, ln_b, w_s, bs_full, w_out, g, b)


PACKED_ROWS = 2 * SUBLANES


def _slab_tokens(v, zero):
    u = pltpu.bitcast(v, jnp.uint32)
    tokens = []
    for j in range(u.shape[0] // SUBLANES):
        rows = u[SUBLANES * j:SUBLANES * (j + 1), :]
        t = rows[:, 0:LANES]
        for l in range(1, u.shape[1] // LANES):
            t = t | rows[:, LANES * l:LANES * (l + 1)]
        tokens.append(t & zero)
    return tokens


def _tie_to_tokens(x, tokens, k_tile):
    u = pltpu.bitcast(x, jnp.uint32)
    n_slabs = u.shape[0] // SUBLANES
    n_lanes = u.shape[1] // LANES
    pieces = [[u[SUBLANES * i:SUBLANES * (i + 1), LANES * l:LANES * (l + 1)] for l in range(n_lanes)]
              for i in range(n_slabs)]
    n_pos = (u.shape[1] // k_tile) * n_slabs
    for j, tok in enumerate(tokens):
        p = j * n_pos // len(tokens)
        k, i = p // n_slabs, p % n_slabs
        l = k * (k_tile // LANES)
        pieces[i][l] = pieces[i][l] | tok
    u = jnp.concatenate([jnp.concatenate(row, axis=1) for row in pieces], axis=0)
    return pltpu.bitcast(u, _BF16)


def _ffn_kernel(x_ref, wup_ref, cw_ref, cb_ref, wdown_ref, g_ref, b_ref, zero_ref, out_ref,
                acc_ref, tail_ref, xb_ref, a0_ref, a1_ref, a2_ref, a3_ref, gated0_ref, gated1_ref,
                *, tm, rs, n_chunks, alpha):
    a_refs = (a0_ref, a1_ref, a2_ref, a3_ref)
    gated_refs = (gated0_ref, gated1_ref)
    n_slabs = tm // rs
    n_items = n_chunks * n_slabs
    xb_ref[...] = x_ref[0].astype(_BF16)
    acc_ref[...] = jnp.zeros_like(acc_ref)

    @pl.when(pl.program_id(1) == 0)
    def _():
        tail_ref[...] = jnp.zeros_like(tail_ref)

    def item(t):
        return t // n_slabs, pl.multiple_of((t % n_slabs) * rs, rs)

    def up(t, slot, tokens=()):
        c, r0 = item(t)
        xs = xb_ref[pl.ds(r0, rs), :]
        if tokens:
            xs = _tie_to_tokens(xs, tokens, MXU_TILE)
        for half in range(2):
            a_refs[slot][half] = _dot(xs, wup_ref[half * n_chunks + c])

    def conv_gate(t, slot_a, slot_g):
        c, _ = item(t)

        def conv_half(half):
            idx = half * n_chunks + c
            a = a_refs[slot_a][half]
            prev = tail_ref[idx]
            tail_ref[idx] = a[rs - SUBLANES:, :]
            ext = jnp.concatenate([prev, a], axis=0)
            a1 = pltpu.roll(ext, 1, axis=0)[SUBLANES:, :]
            a2 = pltpu.roll(ext, 2, axis=0)[SUBLANES:, :]
            w = cw_ref[idx]
            return cb_ref[idx] + w[0:1, :] * a2 + w[1:2, :] * a1 + w[2:3, :] * a

        gate = conv_half(0)
        val = conv_half(1)
        gated = (gate * jax.nn.sigmoid(gate) * val).astype(_BF16)
        gated_refs[slot_g][...] = gated
        return _slab_tokens(gated, zero_ref[...])

    def down(t, slot):
        c, r0 = item(t)
        acc_ref[pl.ds(r0, rs), :] += _dot(gated_refs[slot][...], wdown_ref[c])

    def stage(t, k, do_down=True, do_up=True):
        tokens = conv_gate(t, k % 4, k % 2)
        if do_down:
            down(t - 1, (k - 1) % 2)
        if do_up:
            up(t + 2, (k + 2) % 4, tokens)

    up(0, 0)
    up(1, 1)
    stage(0, 0, do_down=False)
    stage(1, 1)

    def four(q, carry):
        t = 4 * q + 2
        for k in range(4):
            stage(t + k, (2 + k) % 4)
        return carry

    lax.fori_loop(0, (n_items - 4) // 4, four, 0)
    stage(n_items - 2, (n_items - 2) % 4, do_up=False)
    stage(n_items - 1, (n_items - 1) % 4, do_up=False)
    down(n_items - 1, (n_items - 1) % 2)
    out_ref[0] = _layer_norm(alpha * x_ref[0] + acc_ref[...], g_ref[...], b_ref[...])


def _conv_ffn(h, wup_c, cw_c, cb_c, wdown_c, g, b, alpha):
    B, S, D = h.shape
    n2, _, fc = wup_c.shape
    n_chunks = n2 // 2
    tm = min(FFN_ROWS, S)
    rs = min(FFN_SLAB, tm)
    n_items = n_chunks * (tm // rs)
    assert n_items >= 8 and n_items % 4 == 0
    const2 = lambda b, s: (0, 0)
    const3 = lambda b, s: (0, 0, 0)
    return pl.pallas_call(
        functools.partial(_ffn_kernel, tm=tm, rs=rs, n_chunks=n_chunks, alpha=alpha),
        grid=(B, S // tm),
        in_specs=[
            pl.BlockSpec((1, tm, D), lambda b, s: (b, s, 0)),
            _resident((n2, D, fc), const3),
            _resident((n2, CONV_WIDTH, fc), const3),
            _resident((n2, 1, fc), const3),
            _resident((n_chunks, fc, D), const3),
            _resident((1, D), const2),
            _resident((1, D), const2),
            _resident((SUBLANES, LANES), const2),
        ],
        out_specs=pl.BlockSpec((1, tm, D), lambda b, s: (b, s, 0)),
        out_shape=jax.ShapeDtypeStruct((B, S, D), _F32),
        scratch_shapes=[
            pltpu.VMEM((tm, D), _F32),
            pltpu.VMEM((n2, SUBLANES, fc), _F32),
            pltpu.VMEM((tm, D), _BF16),
            pltpu.VMEM((2, rs, fc), _F32),
            pltpu.VMEM((2, rs, fc), _F32),
            pltpu.VMEM((2, rs, fc), _F32),
            pltpu.VMEM((2, rs, fc), _F32),
            pltpu.VMEM((rs, fc), _BF16),
            pltpu.VMEM((rs, fc), _BF16),
        ],
        compiler_params=pltpu.CompilerParams(
            dimension_semantics=("parallel", "arbitrary"), vmem_limit_bytes=VMEM_LIMIT_BYTES),
        name="conv_ffn",
    )(h, wup_c, cw_c, cb_c, wdown_c, g, b, jnp.zeros((SUBLANES, LANES), jnp.uint32))


GMLP_ROWS = 512
GMLP_SUBTILE = 256
FFN_ROWS = 1024
FFN_SLAB = 256
FFN_CHUNK = 256


def _chunk_cols(w, fc):
    r, c = w.shape
    return jnp.transpose(w.reshape(r, c // fc, fc), (1, 0, 2))


def kernel(x, attn_w_in, attn_w_out, gmlp_w_in, gmlp_ln_g, gmlp_ln_b, gmlp_w_s, gmlp_b_s, gmlp_w_out,
           ffn_w_up, ffn_conv_w, ffn_conv_b, ffn_w_down, ln_mix_g, ln_mix_b, ln_ffn_g, ln_ffn_b):
    B, S, D = x.shape
    depth = ffn_w_up.shape[0]
    alpha = (2 * depth) ** 0.25
    d_ff = ffn_w_down.shape[1]
    width = gmlp_w_out.shape[1]
    row = lambda v: v.reshape(1, -1)

    h = x
    for i in range(depth):
        j = i // 2
        if i % 2 == 0:
            w_in = attn_w_in[j]
            wq = w_in[:, :D].astype(_BF16)
            wk = w_in[:, D:2 * D].astype(_BF16)
            wvt = w_in[:, 2 * D:].T.astype(_BF16)
            q, kp, vt = _qkv_proj(h, wq, wk, wvt)
            o = _sb_attention(q, kp, vt)
            h = _mix_out_ln(o, h, attn_w_out[j].astype(_BF16), row(ln_mix_g[i]), row(ln_mix_b[i]), alpha)
        else:
            bs_full = jnp.repeat(gmlp_b_s[j].T, width // GMLP_GROUPS, axis=1)
            h = _gmlp_mixer(h, gmlp_w_in[j].astype(_BF16), row(gmlp_ln_g[j]), row(gmlp_ln_b[j]),
                            gmlp_w_s[j], bs_full, gmlp_w_out[j].astype(_BF16),
                            row(ln_mix_g[i]), row(ln_mix_b[i]), alpha)
        wup_c = _chunk_cols(ffn_w_up[i].astype(_BF16), FFN_CHUNK)
        cw_c = _chunk_cols(ffn_conv_w[i], FFN_CHUNK)
        cb_c = _chunk_cols(ffn_conv_b[i].reshape(1, -1), FFN_CHUNK)
        wdown_c = ffn_w_down[i].astype(_BF16).reshape(d_ff // FFN_CHUNK, FFN_CHUNK, D)
        h = _conv_ffn(h, wup_c, cw_c, cb_c, wdown_c, row(ln_ffn_g[i]), row(ln_ffn_b[i]), alpha)
    return h
```

```python
import functools
import math

import jax
import jax.numpy as jnp
from jax import lax
from jax.experimental import pallas as pl
from jax.experimental.pallas import tpu as pltpu

LN_EPS = 1e-5
CONV_WIDTH = 3
SB_HEAD_DIM = 64
GMLP_GROUPS = 8
GMLP_CHUNK = 128

SUBLANES = 8
LANES = 128
MXU_TILE = 256
KEY_BLOCK = 128
KEY_GROUP = KEY_BLOCK // SUBLANES
VMEM_LIMIT_BYTES = 56 * 1024 * 1024

ATT_TQ = 256
ATT_KB = 256
ATT_LANES = 1024
ATT_CHAIN = 3
LOG2E = 1.0 / math.log(2.0)
INV_LN2 = 1.0 / math.log(2.0)
LOG2_DEAD = 150.0

_BF16 = jnp.bfloat16
_F32 = jnp.float32


def _resident(block_shape, index_map):
    return pl.BlockSpec(block_shape, index_map, pipeline_mode=pl.Buffered(1))


def _layer_norm(y, g, b):
    mu = jnp.mean(y, axis=-1, keepdims=True)
    d = y - mu
    var = jnp.mean(d * d, axis=-1, keepdims=True)
    return d * lax.rsqrt(var + LN_EPS) * g + b


def _dot(a, b):
    return jnp.dot(a, b, preferred_element_type=_F32)


def _dot_nt(a, b):
    return lax.dot_general(a, b, (((1,), (1,)), ((), ())), preferred_element_type=_F32)


def _qkv_kernel(h_ref, wq_ref, wk_ref, wvt_ref, q_ref, kp_ref, vt_ref, *, tm, scale):
    hb = h_ref[0].astype(_BF16)
    row = lax.broadcasted_iota(jnp.int32, (KEY_BLOCK, KEY_BLOCK), 0)
    col = lax.broadcasted_iota(jnp.int32, (KEY_BLOCK, KEY_BLOCK), 1)
    perm = (col == (row % SUBLANES) * KEY_GROUP + row // SUBLANES).astype(_BF16)
    hp = jnp.concatenate(
        [_dot(perm, hb[blk * KEY_BLOCK:(blk + 1) * KEY_BLOCK, :]) for blk in range(tm // KEY_BLOCK)],
        axis=0).astype(_BF16)
    q_ref[0] = (_dot(hb, wq_ref[...]) * scale).astype(_BF16)
    kp_ref[0] = _dot(hp, wk_ref[...]).astype(_BF16)
    vt_ref[0] = _dot_nt(wvt_ref[...], hp).astype(_BF16)


def _qkv_proj(h, wq, wk, wvt):
    B, S, D = h.shape
    tm = min(512, S)
    scale = SB_HEAD_DIM ** -0.5 * LOG2E
    return pl.pallas_call(
        functools.partial(_qkv_kernel, tm=tm, scale=scale),
        grid=(B, S // tm),
        in_specs=[
            pl.BlockSpec((1, tm, D), lambda b, s: (b, s, 0)),
            _resident((D, D), lambda b, s: (0, 0)),
            _resident((D, D), lambda b, s: (0, 0)),
            _resident((D, D), lambda b, s: (0, 0)),
        ],
        out_specs=[
            pl.BlockSpec((1, tm, D), lambda b, s: (b, s, 0)),
            pl.BlockSpec((1, tm, D), lambda b, s: (b, s, 0)),
            pl.BlockSpec((1, D, tm), lambda b, s: (b, 0, s)),
        ],
        out_shape=[
            jax.ShapeDtypeStruct((B, S, D), _BF16),
            jax.ShapeDtypeStruct((B, S, D), _BF16),
            jax.ShapeDtypeStruct((B, D, S), _BF16),
        ],
        compiler_params=pltpu.CompilerParams(
            dimension_semantics=("parallel", "parallel"), vmem_limit_bytes=VMEM_LIMIT_BYTES),
        name="qkv_proj",
    )(h, wq, wk, wvt)


def _sublane_suffix_scan(g):
    row = lax.broadcasted_iota(jnp.int32, g.shape, 0)
    x = g
    for sh in (1, 2, 4):
        shifted = pltpu.roll(x, SUBLANES - sh, axis=0)
        x = x + jnp.where(row + sh < SUBLANES, shifted, 0.0)
    return x


def _sb_block(z, carry, causal):
    m = jnp.maximum(z, 0.0) + jnp.log(1.0 + jnp.exp2(-jnp.abs(z))) * INV_LN2
    if causal is not None:
        m = jnp.where(causal, m, 0.0)
    tiles = [m[SUBLANES * v:SUBLANES * (v + 1), :] for v in range(KEY_GROUP)]
    run = [None] * KEY_GROUP
    run[KEY_GROUP - 1] = tiles[KEY_GROUP - 1]
    for v in range(KEY_GROUP - 2, -1, -1):
        run[v] = run[v + 1] + tiles[v]
    group_tot = run[0]
    incl = _sublane_suffix_scan(group_tot)
    base = (incl - group_tot) + carry
    total = jnp.concatenate([run[v] + base for v in range(KEY_GROUP)], axis=0)
    a = jnp.exp2(z - total)
    if causal is not None:
        a = jnp.where(causal, a, 0.0)
    new_carry = carry + jnp.broadcast_to(incl[0:1, :], carry.shape)
    return a.astype(_BF16), new_carry


def _attn_kernel(q_ref, kp_ref, vt_ref, o_ref, acc_ref, carry_ref, *, tq, n_heads):
    qi = pl.program_id(2)
    lane = lax.broadcasted_iota(jnp.int32, (tq, LANES), 1)
    qm = []
    for h in range(n_heads):
        qpair = q_ref[0, :, LANES * (h // 2):LANES * (h // 2 + 1)]
        qm.append(jnp.where((lane // SB_HEAD_DIM) == h % 2, qpair, jnp.zeros_like(qpair)))

    acc_ref[...] = jnp.zeros_like(acc_ref)
    carry_ref[...] = jnp.zeros_like(carry_ref)

    row = lax.broadcasted_iota(jnp.int32, (KEY_BLOCK, tq), 0)
    key_off = (row % SUBLANES) * KEY_GROUP + row // SUBLANES
    query_off = lax.broadcasted_iota(jnp.int32, (KEY_BLOCK, tq), 1)

    def key_step(j, diagonal):
        n_sub = ATT_KB // KEY_BLOCK

        def scores(h):
            out = [None] * n_sub
            for sub in range(n_sub - 1, -1, -1):
                start = pl.multiple_of(j * ATT_KB + sub * KEY_BLOCK, KEY_BLOCK)
                kblk = kp_ref[0, pl.ds(start, KEY_BLOCK), LANES * (h // 2):LANES * (h // 2 + 1)]
                out[sub] = _dot_nt(kblk, qm[h])
            return out

        z = {h: scores(h) for h in range(min(ATT_CHAIN, n_heads))}
        for h in range(n_heads):
            a_parts = [None] * n_sub
            for sub in range(n_sub - 1, -1, -1):
                mask = (key_off + sub * KEY_BLOCK < query_off) if diagonal else None
                a_parts[sub], carry_ref[h] = _sb_block(z[h][sub], carry_ref[h], mask)
            if h + ATT_CHAIN < n_heads:
                z[h + ATT_CHAIN] = scores(h + ATT_CHAIN)
            vt = vt_ref[0, SB_HEAD_DIM * h:SB_HEAD_DIM * (h + 1),
                        pl.ds(pl.multiple_of(j * ATT_KB, ATT_KB), ATT_KB)]
            acc_ref[SB_HEAD_DIM * h:SB_HEAD_DIM * (h + 1), :] += _dot(vt, jnp.concatenate(a_parts, axis=0))

    def any_alive():
        return jnp.min(carry_ref[...]) < LOG2_DEAD

    key_step(qi, True)

    def body(state):
        j, _ = state
        key_step(j, False)
        return j - 1, any_alive()

    lax.while_loop(lambda st: jnp.logical_and(st[0] >= 0, st[1]), body, (qi - 1, any_alive()))
    o_ref[0] = acc_ref[...].T.astype(o_ref.dtype)


def _sb_attention(q, kp, vt):
    B, S, D = q.shape
    tq = min(ATT_TQ, S)
    assert tq == ATT_KB and S % tq == 0 and D % ATT_LANES == 0
    n_heads = ATT_LANES // SB_HEAD_DIM
    return pl.pallas_call(
        functools.partial(_attn_kernel, tq=tq, n_heads=n_heads),
        grid=(B, D // ATT_LANES, S // tq),
        in_specs=[
            pl.BlockSpec((1, tq, ATT_LANES), lambda b, h, i: (b, i, h)),
            pl.BlockSpec((1, S, ATT_LANES), lambda b, h, i: (b, 0, h)),
            pl.BlockSpec((1, ATT_LANES, S), lambda b, h, i: (b, h, 0)),
        ],
        out_specs=pl.BlockSpec((1, tq, ATT_LANES), lambda b, h, i: (b, i, h)),
        out_shape=jax.ShapeDtypeStruct((B, S, D), _BF16),
        scratch_shapes=[
            pltpu.VMEM((ATT_LANES, tq), _F32),
            pltpu.VMEM((n_heads, SUBLANES, tq), _F32),
        ],
        compiler_params=pltpu.CompilerParams(
            dimension_semantics=("parallel", "parallel", "parallel"),
            vmem_limit_bytes=VMEM_LIMIT_BYTES),
        name="sb_attention",
    )(q, kp, vt)


def _mix_out_kernel(o_ref, h_ref, w_ref, g_ref, b_ref, out_ref, *, alpha):
    m = _dot(o_ref[0], w_ref[...])
    out_ref[0] = _layer_norm(alpha * h_ref[0] + m, g_ref[...], b_ref[...])


def _mix_out_ln(o, h, w_out, g, b, alpha):
    B, S, D = h.shape
    tm = min(512, S)
    return pl.pallas_call(
        functools.partial(_mix_out_kernel, alpha=alpha),
        grid=(B, S // tm),
        in_specs=[
            pl.BlockSpec((1, tm, D), lambda b, s: (b, s, 0)),
            pl.BlockSpec((1, tm, D), lambda b, s: (b, s, 0)),
            _resident((D, D), lambda b, s: (0, 0)),
            _resident((1, D), lambda b, s: (0, 0)),
            _resident((1, D), lambda b, s: (0, 0)),
        ],
        out_specs=pl.BlockSpec((1, tm, D), lambda b, s: (b, s, 0)),
        out_shape=jax.ShapeDtypeStruct((B, S, D), _F32),
        compiler_params=pltpu.CompilerParams(
            dimension_semantics=("parallel", "parallel"), vmem_limit_bytes=VMEM_LIMIT_BYTES),
        name="mix_out_ln",
    )(o, h, w_out, g, b)


def _gelu_tanh(x):
    c = math.sqrt(2.0 / math.pi)
    return 0.5 * x * (1.0 + jnp.tanh(c * (x + 0.044715 * (x * x * x))))


def _gmlp_kernel(x_ref, win_ref, lng_ref, lnb_ref, ws_ref, bs_ref, wout_ref, g_ref, b_ref,
                 out_ref, *, tm, rt, width, alpha):
    gw = width // GMLP_GROUPS
    tri = (lax.broadcasted_iota(jnp.int32, (GMLP_CHUNK, GMLP_CHUNK), 0)
           >= lax.broadcasted_iota(jnp.int32, (GMLP_CHUNK, GMLP_CHUNK), 1))
    w_causal = [jnp.where(tri, ws_ref[g], 0.0).astype(_BF16) for g in range(GMLP_GROUPS)]

    def proj_in(i):
        xb = x_ref[0, i * rt:(i + 1) * rt, :].astype(_BF16)
        return _dot(xb, win_ref[:, :width]), _dot(xb, win_ref[:, width:])

    zz = proj_in(0)
    for i in range(tm // rt):
        nxt = proj_in(i + 1) if (i + 1) * rt < tm else None
        u = _gelu_tanh(zz[0])
        v = _gelu_tanh(zz[1])
        vn = _layer_norm(v, lng_ref[...], lnb_ref[...]).astype(_BF16)
        chunks = []
        for ch in range(rt // GMLP_CHUNK):
            r0 = ch * GMLP_CHUNK
            cols = []
            for g in range(GMLP_GROUPS):
                cols.append(_dot(w_causal[g], vn[r0:r0 + GMLP_CHUNK, g * gw:(g + 1) * gw]))
            chunks.append(jnp.concatenate(cols, axis=1) + bs_ref[...])
        s = jnp.concatenate(chunks, axis=0)
        m = _dot((u * s).astype(_BF16), wout_ref[...])
        rows = slice(i * rt, (i + 1) * rt)
        out_ref[0, rows, :] = _layer_norm(alpha * x_ref[0, rows, :] + m, g_ref[...], b_ref[...])
        zz = nxt


def _gmlp_mixer(h, w_in, ln_g, ln_b, w_s, bs_full, w_out, g, b, alpha):
    B, S, D = h.shape
    width = w_out.shape[0]
    tm = min(GMLP_ROWS, S)
    rt = min(GMLP_SUBTILE, tm)
    const2 = lambda b, s: (0, 0)
    return pl.pallas_call(
        functools.partial(_gmlp_kernel, tm=tm, rt=rt, width=width, alpha=alpha),
        grid=(B, S // tm),
        in_specs=[
            pl.BlockSpec((1, tm, D), lambda b, s: (b, s, 0)),
            _resident((D, 2 * width), const2),
            _resident((1, width), const2),
            _resident((1, width), const2),
            _resident((GMLP_GROUPS, GMLP_CHUNK, GMLP_CHUNK), lambda b, s: (0, 0, 0)),
            _resident((GMLP_CHUNK, width), const2),
            _resident((width, D), const2),
            _resident((1, D), const2),
            _resident((1, D), const2),
        ],
        out_specs=pl.BlockSpec((1, tm, D), lambda b, s: (b, s, 0)),
        out_shape=jax.ShapeDtypeStruct((B, S, D), _F32),
        compiler_params=pltpu.CompilerParams(
            dimension_semantics=("parallel", "parallel"), vmem_limit_bytes=VMEM_LIMIT_BYTES),
        name="gmlp_mixer",
    )(h, w_in, ln_g, ln_b, w_s, bs_full, w_out, g, b)


PACKED_ROWS = 2 * SUBLANES


def _slab_tokens(v, zero):
    u = pltpu.bitcast(v, jnp.uint32)
    tokens = []
    for j in range(u.shape[0] // SUBLANES):
        rows = u[SUBLANES * j:SUBLANES * (j + 1), :]
        t = rows[:, 0:LANES]
        for l in range(1, u.shape[1] // LANES):
            t = t | rows[:, LANES * l:LANES * (l + 1)]
        tokens.append(t & zero)
    return tokens


def _tie_to_tokens(x, tokens, k_tile):
    n_slabs = x.shape[0] // PACKED_ROWS
    n_lanes = x.shape[1] // LANES
    pieces = [[x[PACKED_ROWS * i:PACKED_ROWS * (i + 1), LANES * l:LANES * (l + 1)] for l in range(n_lanes)]
              for i in range(n_slabs)]
    n_pos = (x.shape[1] // k_tile) * n_slabs
    for j, tok in enumerate(tokens):
        p = j * n_pos // len(tokens)
        k, i = p // n_slabs, p % n_slabs
        l = k * (k_tile // LANES)
        pieces[i][l] = pieces[i][l] + pltpu.bitcast(tok, _BF16)
    return jnp.concatenate([jnp.concatenate(row, axis=1) for row in pieces], axis=0)


def _ffn_kernel(x_ref, wup_ref, cw_ref, cb_ref, wdown_ref, g_ref, b_ref, zero_ref, out_ref,
                acc_ref, tail_ref, xb_ref, a0_ref, a1_ref, a2_ref, a3_ref, gated0_ref, gated1_ref,
                *, tm, rs, n_chunks, alpha):
    a_refs = (a0_ref, a1_ref, a2_ref, a3_ref)
    gated_refs = (gated0_ref, gated1_ref)
    n_slabs = tm // rs
    n_items = n_chunks * n_slabs
    xb_ref[...] = x_ref[0].astype(_BF16)
    acc_ref[...] = jnp.zeros_like(acc_ref)

    @pl.when(pl.program_id(1) == 0)
    def _():
        tail_ref[...] = jnp.zeros_like(tail_ref)

    def item(t):
        return t // n_slabs, pl.multiple_of((t % n_slabs) * rs, rs)

    def up(t, slot, tokens=()):
        c, r0 = item(t)
        xs = xb_ref[pl.ds(r0, rs), :]
        if tokens:
            xs = _tie_to_tokens(xs, tokens, MXU_TILE)
        for half in range(2):
            a_refs[slot][half] = _dot(xs, wup_ref[half * n_chunks + c])

    def conv_gate(t, slot_a, slot_g):
        c, _ = item(t)

        def conv_half(half):
            idx = half * n_chunks + c
            a = a_refs[slot_a][half]
            prev = tail_ref[idx]
            tail_ref[idx] = a[rs - SUBLANES:, :]
            ext = jnp.concatenate([prev, a], axis=0)
            a1 = pltpu.roll(ext, 1, axis=0)[SUBLANES:, :]
            a2 = pltpu.roll(ext, 2, axis=0)[SUBLANES:, :]
            w = cw_ref[idx]
            return cb_ref[idx] + w[0:1, :] * a2 + w[1:2, :] * a1 + w[2:3, :] * a

        gate = conv_half(0)
        val = conv_half(1)
        gated = (gate * jax.nn.sigmoid(gate) * val).astype(_BF16)
        gated_refs[slot_g][...] = gated
        return _slab_tokens(gated, zero_ref[...])

    def down(t, slot):
        c, r0 = item(t)
        acc_ref[pl.ds(r0, rs), :] += _dot(gated_refs[slot][...], wdown_ref[c])

    def stage(t, k, do_down=True, do_up=True):
        tokens = conv_gate(t, k % 4, k % 2)
        if do_down:
            down(t - 1, (k - 1) % 2)
        if do_up:
            up(t + 2, (k + 2) % 4, tokens)

    up(0, 0)
    up(1, 1)
    stage(0, 0, do_down=False)
    stage(1, 1)

    def four(q, carry):
        t = 4 * q + 2
        for k in range(4):
            stage(t + k, (2 + k) % 4)
        return carry

    lax.fori_loop(0, (n_items - 4) // 4, four, 0)
    stage(n_items - 2, (n_items - 2) % 4, do_up=False)
    stage(n_items - 1, (n_items - 1) % 4, do_up=False)
    down(n_items - 1, (n_items - 1) % 2)
    out_ref[0] = _layer_norm(alpha * x_ref[0] + acc_ref[...], g_ref[...], b_ref[...])


def _conv_ffn(h, wup_c, cw_c, cb_c, wdown_c, g, b, alpha):
    B, S, D = h.shape
    n2, _, fc = wup_c.shape
    n_chunks = n2 // 2
    tm = min(FFN_ROWS, S)
    rs = min(FFN_SLAB, tm)
    n_items = n_chunks * (tm // rs)
    assert n_items >= 8 and n_items % 4 == 0
    const2 = lambda b, s: (0, 0)
    const3 = lambda b, s: (0, 0, 0)
    return pl.pallas_call(
        functools.partial(_ffn_kernel, tm=tm, rs=rs, n_chunks=n_chunks, alpha=alpha),
        grid=(B, S // tm),
        in_specs=[
            pl.BlockSpec((1, tm, D), lambda b, s: (b, s, 0)),
            _resident((n2, D, fc), const3),
            _resident((n2, CONV_WIDTH, fc), const3),
            _resident((n2, 1, fc), const3),
            _resident((n_chunks, fc, D), const3),
            _resident((1, D), const2),
            _resident((1, D), const2),
            _resident((SUBLANES, LANES), const2),
        ],
        out_specs=pl.BlockSpec((1, tm, D), lambda b, s: (b, s, 0)),
        out_shape=jax.ShapeDtypeStruct((B, S, D), _F32),
        scratch_shapes=[
            pltpu.VMEM((tm, D), _F32),
            pltpu.VMEM((n2, SUBLANES, fc), _F32),
            pltpu.VMEM((tm, D), _BF16),
            pltpu.VMEM((2, rs, fc), _F32),
            pltpu.VMEM((2, rs, fc), _F32),
            pltpu.VMEM((2, rs, fc), _F32),
            pltpu.VMEM((2, rs, fc), _F32),
            pltpu.VMEM((rs, fc), _BF16),
            pltpu.VMEM((rs, fc), _BF16),
        ],
        compiler_params=pltpu.CompilerParams(
            dimension_semantics=("parallel", "arbitrary"), vmem_limit_bytes=VMEM_LIMIT_BYTES),
        name="conv_ffn",
    )(h, wup_c, cw_c, cb_c, wdown_c, g, b, jnp.zeros((SUBLANES, LANES), jnp.uint32))


GMLP_ROWS = 512
GMLP_SUBTILE = 256
FFN_ROWS = 1024
FFN_SLAB = 256
FFN_CHUNK = 256


def _chunk_cols(w, fc):
    r, c = w.shape
    return jnp.transpose(w.reshape(r, c // fc, fc), (1, 0, 2))


def kernel(x, attn_w_in, attn_w_out, gmlp_w_in, gmlp_ln_g, gmlp_ln_b, gmlp_w_s, gmlp_b_s, gmlp_w_out,
           ffn_w_up, ffn_conv_w, ffn_conv_b, ffn_w_down, ln_mix_g, ln_mix_b, ln_ffn_g, ln_ffn_b):
    B, S, D = x.shape
    depth = ffn_w_up.shape[0]
    alpha = (2 * depth) ** 0.25
    d_ff = ffn_w_down.shape[1]
    width = gmlp_w_out.shape[1]
    row = lambda v: v.reshape(1, -1)

    h = x
    for i in range(depth):
        j = i // 2
        if i % 2 == 0:
            w_in = attn_w_in[j]
            wq = w_in[:, :D].astype(_BF16)
            wk = w_in[:, D:2 * D].astype(_BF16)
            wvt = w_in[:, 2 * D:].T.astype(_BF16)
            q, kp, vt = _qkv_proj(h, wq, wk, wvt)
            o = _sb_attention(q, kp, vt)
            h = _mix_out_ln(o, h, attn_w_out[j].astype(_BF16), row(ln_mix_g[i]), row(ln_mix_b[i]), alpha)
        else:
            bs_full = jnp.repeat(gmlp_b_s[j].T, width // GMLP_GROUPS, axis=1)
            h = _gmlp_mixer(h, gmlp_w_in[j].astype(_BF16), row(gmlp_ln_g[j]), row(gmlp_ln_b[j]),
                            gmlp_w_s[j], bs_full, gmlp_w_out[j].astype(_BF16),
                            row(ln_mix_g[i]), row(ln_mix_b[i]), alpha)
        wup_c = _chunk_cols(ffn_w_up[i].astype(_BF16), FFN_CHUNK)
        cw_c = _chunk_cols(ffn_conv_w[i], FFN_CHUNK)
        cb_c = _chunk_cols(ffn_conv_b[i].reshape(1, -1), FFN_CHUNK)
        wdown_c = ffn_w_down[i].astype(_BF16).reshape(d_ff // FFN_CHUNK, FFN_CHUNK, D)
        h = _conv_ffn(h, wup_c, cw_c, cb_c, wdown_c, row(ln_ffn_g[i]), row(ln_ffn_b[i]), alpha)
    return h
```

```python
import functools
import math

import jax
import jax.numpy as jnp
from jax import lax
from jax.experimental import pallas as pl
from jax.experimental.pallas import tpu as pltpu

LN_EPS = 1e-5
CONV_WIDTH = 3
SB_HEAD_DIM = 64
GMLP_GROUPS = 8
GMLP_CHUNK = 128

SUBLANES = 8
LANES = 128
MXU_TILE = 256
KEY_BLOCK = 128
KEY_GROUP = KEY_BLOCK // SUBLANES
VMEM_LIMIT_BYTES = 56 * 1024 * 1024

ATT_TQ = 256
ATT_KB = 256
ATT_LANES = 1024
ATT_CHAIN = 3
LOG2E = 1.0 / math.log(2.0)
INV_LN2 = 1.0 / math.log(2.0)
LOG2_DEAD = 150.0

_BF16 = jnp.bfloat16
_F32 = jnp.float32


def _resident(block_shape, index_map):
    return pl.BlockSpec(block_shape, index_map, pipeline_mode=pl.Buffered(1))


def _layer_norm(y, g, b):
    mu = jnp.mean(y, axis=-1, keepdims=True)
    d = y - mu
    var = jnp.mean(d * d, axis=-1, keepdims=True)
    return d * lax.rsqrt(var + LN_EPS) * g + b


def _dot(a, b):
    return jnp.dot(a, b, preferred_element_type=_F32)


def _dot_nt(a, b):
    return lax.dot_general(a, b, (((1,), (1,)), ((), ())), preferred_element_type=_F32)


def _qkv_kernel(h_ref, wq_ref, wk_ref, wvt_ref, q_ref, kp_ref, vt_ref, *, tm, scale):
    hb = h_ref[0].astype(_BF16)
    row = lax.broadcasted_iota(jnp.int32, (KEY_BLOCK, KEY_BLOCK), 0)
    col = lax.broadcasted_iota(jnp.int32, (KEY_BLOCK, KEY_BLOCK), 1)
    perm = (col == (row % SUBLANES) * KEY_GROUP + row // SUBLANES).astype(_BF16)
    hp = jnp.concatenate(
        [_dot(perm, hb[blk * KEY_BLOCK:(blk + 1) * KEY_BLOCK, :]) for blk in range(tm // KEY_BLOCK)],
        axis=0).astype(_BF16)
    q_ref[0] = (_dot(hb, wq_ref[...]) * scale).astype(_BF16)
    kp_ref[0] = _dot(hp, wk_ref[...]).astype(_BF16)
    vt_ref[0] = _dot_nt(wvt_ref[...], hp).astype(_BF16)


def _qkv_proj(h, wq, wk, wvt):
    B, S, D = h.shape
    tm = min(512, S)
    scale = SB_HEAD_DIM ** -0.5 * LOG2E
    return pl.pallas_call(
        functools.partial(_qkv_kernel, tm=tm, scale=scale),
        grid=(B, S // tm),
        in_specs=[
            pl.BlockSpec((1, tm, D), lambda b, s: (b, s, 0)),
            _resident((D, D), lambda b, s: (0, 0)),
            _resident((D, D), lambda b, s: (0, 0)),
            _resident((D, D), lambda b, s: (0, 0)),
        ],
        out_specs=[
            pl.BlockSpec((1, tm, D), lambda b, s: (b, s, 0)),
            pl.BlockSpec((1, tm, D), lambda b, s: (b, s, 0)),
            pl.BlockSpec((1, D, tm), lambda b, s: (b, 0, s)),
        ],
        out_shape=[
            jax.ShapeDtypeStruct((B, S, D), _BF16),
            jax.ShapeDtypeStruct((B, S, D), _BF16),
            jax.ShapeDtypeStruct((B, D, S), _BF16),
        ],
        compiler_params=pltpu.CompilerParams(
            dimension_semantics=("parallel", "parallel"), vmem_limit_bytes=VMEM_LIMIT_BYTES),
        name="qkv_proj",
    )(h, wq, wk, wvt)


def _sublane_suffix_scan(g):
    row = lax.broadcasted_iota(jnp.int32, g.shape, 0)
    x = g
    for sh in (1, 2, 4):
        shifted = pltpu.roll(x, SUBLANES - sh, axis=0)
        x = x + jnp.where(row + sh < SUBLANES, shifted, 0.0)
    return x


def _sb_block(z, carry, causal):
    m = jnp.maximum(z, 0.0) + jnp.log(1.0 + jnp.exp2(-jnp.abs(z))) * INV_LN2
    if causal is not None:
        m = jnp.where(causal, m, 0.0)
    tiles = [m[SUBLANES * v:SUBLANES * (v + 1), :] for v in range(KEY_GROUP)]
    run = [None] * KEY_GROUP
    run[KEY_GROUP - 1] = tiles[KEY_GROUP - 1]
    for v in range(KEY_GROUP - 2, -1, -1):
        run[v] = run[v + 1] + tiles[v]
    group_tot = run[0]
    incl = _sublane_suffix_scan(group_tot)
    base = (incl - group_tot) + carry
    total = jnp.concatenate([run[v] + base for v in range(KEY_GROUP)], axis=0)
    a = jnp.exp2(z - total)
    if causal is not None:
        a = jnp.where(causal, a, 0.0)
    new_carry = carry + jnp.broadcast_to(incl[0:1, :], carry.shape)
    return a.astype(_BF16), new_carry


def _attn_kernel(q_ref, kp_ref, vt_ref, o_ref, acc_ref, carry_ref, *, tq, n_heads):
    qi = pl.program_id(2)
    lane = lax.broadcasted_iota(jnp.int32, (tq, LANES), 1)
    qm = []
    for h in range(n_heads):
        qpair = q_ref[0, :, LANES * (h // 2):LANES * (h // 2 + 1)]
        qm.append(jnp.where((lane // SB_HEAD_DIM) == h % 2, qpair, jnp.zeros_like(qpair)))

    acc_ref[...] = jnp.zeros_like(acc_ref)
    carry_ref[...] = jnp.zeros_like(carry_ref)

    def causal_mask(sub, q0):
        row = lax.broadcasted_iota(jnp.int32, (KEY_BLOCK, tq - q0), 0)
        key_off = (row % SUBLANES) * KEY_GROUP + row // SUBLANES + sub * KEY_BLOCK
        return key_off < lax.broadcasted_iota(jnp.int32, (KEY_BLOCK, tq - q0), 1) + q0

    def key_step(j, diagonal):
        n_sub = ATT_KB // KEY_BLOCK
        first_q = [sub * KEY_BLOCK if diagonal else 0 for sub in range(n_sub)]

        def scores(h):
            out = [None] * n_sub
            for sub in range(n_sub - 1, -1, -1):
                start = pl.multiple_of(j * ATT_KB + sub * KEY_BLOCK, KEY_BLOCK)
                kblk = kp_ref[0, pl.ds(start, KEY_BLOCK), LANES * (h // 2):LANES * (h // 2 + 1)]
                out[sub] = _dot_nt(kblk, qm[h][first_q[sub]:, :])
            return out

        z = {h: scores(h) for h in range(min(ATT_CHAIN, n_heads))}
        for h in range(n_heads):
            a_parts = [None] * n_sub
            for sub in range(n_sub - 1, -1, -1):
                q0 = first_q[sub]
                mask = causal_mask(sub, q0) if diagonal else None
                carry = carry_ref[h]
                a, new_carry = _sb_block(z[h][sub], carry[:, q0:], mask)
                if q0:
                    a = jnp.concatenate([jnp.zeros((KEY_BLOCK, q0), a.dtype), a], axis=1)
                    new_carry = jnp.concatenate([carry[:, :q0], new_carry], axis=1)
                a_parts[sub], carry_ref[h] = a, new_carry
            if h + ATT_CHAIN < n_heads:
                z[h + ATT_CHAIN] = scores(h + ATT_CHAIN)
            vt = vt_ref[0, SB_HEAD_DIM * h:SB_HEAD_DIM * (h + 1),
                        pl.ds(pl.multiple_of(j * ATT_KB, ATT_KB), ATT_KB)]
            acc_ref[SB_HEAD_DIM * h:SB_HEAD_DIM * (h + 1), :] += _dot(vt, jnp.concatenate(a_parts, axis=0))

    def any_alive():
        return jnp.min(carry_ref[...]) < LOG2_DEAD

    key_step(qi, True)

    def body(state):
        j, _ = state
        key_step(j, False)
        return j - 1, any_alive()

    lax.while_loop(lambda st: jnp.logical_and(st[0] >= 0, st[1]), body, (qi - 1, any_alive()))
    o_ref[0] = acc_ref[...].T.astype(o_ref.dtype)


def _sb_attention(q, kp, vt):
    B, S, D = q.shape
    tq = min(ATT_TQ, S)
    assert tq == ATT_KB and S % tq == 0 and D % ATT_LANES == 0
    n_heads = ATT_LANES // SB_HEAD_DIM
    return pl.pallas_call(
        functools.partial(_attn_kernel, tq=tq, n_heads=n_heads),
        grid=(B, D // ATT_LANES, S // tq),
        in_specs=[
            pl.BlockSpec((1, tq, ATT_LANES), lambda b, h, i: (b, i, h)),
            pl.BlockSpec((1, S, ATT_LANES), lambda b, h, i: (b, 0, h)),
            pl.BlockSpec((1, ATT_LANES, S), lambda b, h, i: (b, h, 0)),
        ],
        out_specs=pl.BlockSpec((1, tq, ATT_LANES), lambda b, h, i: (b, i, h)),
        out_shape=jax.ShapeDtypeStruct((B, S, D), _BF16),
        scratch_shapes=[
            pltpu.VMEM((ATT_LANES, tq), _F32),
            pltpu.VMEM((n_heads, SUBLANES, tq), _F32),
        ],
        compiler_params=pltpu.CompilerParams(
            dimension_semantics=("parallel", "parallel", "parallel"),
            vmem_limit_bytes=VMEM_LIMIT_BYTES),
        name="sb_attention",
    )(q, kp, vt)


def _mix_out_kernel(o_ref, h_ref, w_ref, g_ref, b_ref, out_ref, *, alpha):
    m = _dot(o_ref[0], w_ref[...])
    out_ref[0] = _layer_norm(alpha * h_ref[0] + m, g_ref[...], b_ref[...])


def _mix_out_ln(o, h, w_out, g, b, alpha):
    B, S, D = h.shape
    tm = min(512, S)
    return pl.pallas_call(
        functools.partial(_mix_out_kernel, alpha=alpha),
        grid=(B, S // tm),
        in_specs=[
            pl.BlockSpec((1, tm, D), lambda b, s: (b, s, 0)),
            pl.BlockSpec((1, tm, D), lambda b, s: (b, s, 0)),
            _resident((D, D), lambda b, s: (0, 0)),
            _resident((1, D), lambda b, s: (0, 0)),
            _resident((1, D), lambda b, s: (0, 0)),
        ],
        out_specs=pl.BlockSpec((1, tm, D), lambda b, s: (b, s, 0)),
        out_shape=jax.ShapeDtypeStruct((B, S, D), _F32),
        compiler_params=pltpu.CompilerParams(
            dimension_semantics=("parallel", "parallel"), vmem_limit_bytes=VMEM_LIMIT_BYTES),
        name="mix_out_ln",
    )(o, h, w_out, g, b)


def _gelu_tanh(x):
    c = math.sqrt(2.0 / math.pi)
    return 0.5 * x * (1.0 + jnp.tanh(c * (x + 0.044715 * (x * x * x))))


def _gmlp_kernel(x_ref, win_ref, lng_ref, lnb_ref, ws_ref, bs_ref, wout_ref, g_ref, b_ref,
                 out_ref, *, tm, rt, width, alpha):
    gw = width // GMLP_GROUPS
    tri = (lax.broadcasted_iota(jnp.int32, (GMLP_CHUNK, GMLP_CHUNK), 0)
           >= lax.broadcasted_iota(jnp.int32, (GMLP_CHUNK, GMLP_CHUNK), 1))
    w_causal = [jnp.where(tri, ws_ref[g], 0.0).astype(_BF16) for g in range(GMLP_GROUPS)]

    def proj_in(i):
        xb = x_ref[0, i * rt:(i + 1) * rt, :].astype(_BF16)
        return _dot(xb, win_ref[:, :width]), _dot(xb, win_ref[:, width:])

    zz = proj_in(0)
    for i in range(tm // rt):
        nxt = proj_in(i + 1) if (i + 1) * rt < tm else None
        u = _gelu_tanh(zz[0])
        v = _gelu_tanh(zz[1])
        vn = _layer_norm(v, lng_ref[...], lnb_ref[...]).astype(_BF16)
        chunks = []
        for ch in range(rt // GMLP_CHUNK):
            r0 = ch * GMLP_CHUNK
            cols = []
            for g in range(GMLP_GROUPS):
                cols.append(_dot(w_causal[g], vn[r0:r0 + GMLP_CHUNK, g * gw:(g + 1) * gw]))
            chunks.append(jnp.concatenate(cols, axis=1) + bs_ref[...])
        s = jnp.concatenate(chunks, axis=0)
        m = _dot((u * s).astype(_BF16), wout_ref[...])
        rows = slice(i * rt, (i + 1) * rt)
        out_ref[0, rows, :] = _layer_norm(alpha * x_ref[0, rows, :] + m, g_ref[...], b_ref[...])
        zz = nxt


def _gmlp_mixer(h, w_in, ln_g, ln_b, w_s, bs_full, w_out, g, b, alpha):
    B, S, D = h.shape
    width = w_out.shape[0]
    tm = min(GMLP_ROWS, S)
    rt = min(GMLP_SUBTILE, tm)
    const2 = lambda b, s: (0, 0)
    return pl.pallas_call(
        functools.partial(_gmlp_kernel, tm=tm, rt=rt, width=width, alpha=alpha),
        grid=(B, S // tm),
        in_specs=[
            pl.BlockSpec((1, tm, D), lambda b, s: (b, s, 0)),
            _resident((D, 2 * width), const2),
            _resident((1, width), const2),
            _resident((1, width), const2),
            _resident((GMLP_GROUPS, GMLP_CHUNK, GMLP_CHUNK), lambda b, s: (0, 0, 0)),
            _resident((GMLP_CHUNK, width), const2),
            _resident((width, D), const2),
            _resident((1, D), const2),
            _resident((1, D), const2),
        ],
        out_specs=pl.BlockSpec((1, tm, D), lambda b, s: (b, s, 0)),
        out_shape=jax.ShapeDtypeStruct((B, S, D), _F32),
        compiler_params=pltpu.CompilerParams(
            dimension_semantics=("parallel", "parallel"), vmem_limit_bytes=VMEM_LIMIT_BYTES),
        name="gmlp_mixer",
    )(h, w_in, ln_g, ln_b, w_s, bs_full, w_out, g, b)


PACKED_ROWS = 2 * SUBLANES


def _slab_tokens(v, zero):
    u = pltpu.bitcast(v, jnp.uint32)
    tokens = []
    for j in range(u.shape[0] // SUBLANES):
        for l in range(u.shape[1] // LANES):
            tokens.append(u[SUBLANES * j:SUBLANES * (j + 1), LANES * l:LANES * (l + 1)] & zero)
    return tokens


def _tie_to_tokens(x, tokens, k_tile):
    n_slabs = x.shape[0] // PACKED_ROWS
    n_lanes = x.shape[1] // LANES
    pieces = [[x[PACKED_ROWS * i:PACKED_ROWS * (i + 1), LANES * l:LANES * (l + 1)] for l in range(n_lanes)]
              for i in range(n_slabs)]
    n_pos = (x.shape[1] // k_tile) * n_slabs
    for j, tok in enumerate(tokens):
        p = j * n_pos // len(tokens)
        k, i = p // n_slabs, p % n_slabs
        l = k * (k_tile // LANES)
        pieces[i][l] = pieces[i][l] + pltpu.bitcast(tok, _BF16)
    return jnp.concatenate([jnp.concatenate(row, axis=1) for row in pieces], axis=0)


def _ffn_kernel(x_ref, wup_ref, cw_ref, cb_ref, wdown_ref, g_ref, b_ref, zero_ref, out_ref,
                acc_ref, tail_ref, xb_ref, a0_ref, a1_ref, a2_ref, a3_ref,
                gated0_ref, gated1_ref, gated2_ref, gated3_ref, *, tm, rs, n_chunks, alpha):
    a_refs = (a0_ref, a1_ref, a2_ref, a3_ref)
    gated_refs = (gated0_ref, gated1_ref, gated2_ref, gated3_ref)
    n_slabs = tm // rs
    n_items = n_chunks * n_slabs
    xb_ref[...] = x_ref[0].astype(_BF16)
    acc_ref[...] = jnp.zeros_like(acc_ref)

    @pl.when(pl.program_id(1) == 0)
    def _():
        tail_ref[...] = jnp.zeros_like(tail_ref)

    def item(t):
        return t // n_slabs, pl.multiple_of((t % n_slabs) * rs, rs)

    def up(t, slot, tokens=()):
        c, r0 = item(t)
        xs = xb_ref[pl.ds(r0, rs), :]
        if tokens:
            xs = _tie_to_tokens(xs, tokens, MXU_TILE)
        for half in range(2):
            a_refs[slot][half] = _dot(xs, wup_ref[half * n_chunks + c])

    def conv_gate(t, slot_a, slot_g):
        c, _ = item(t)

        def conv_half(half):
            idx = half * n_chunks + c
            a = a_refs[slot_a][half]
            prev = tail_ref[idx]
            tail_ref[idx] = a[rs - SUBLANES:, :]
            ext = jnp.concatenate([prev, a], axis=0)
            a1 = pltpu.roll(ext, 1, axis=0)[SUBLANES:, :]
            a2 = pltpu.roll(ext, 2, axis=0)[SUBLANES:, :]
            w = cw_ref[idx]
            return cb_ref[idx] + w[0:1, :] * a2 + w[1:2, :] * a1 + w[2:3, :] * a

        gate = conv_half(0)
        val = conv_half(1)
        gated = (gate * jax.nn.sigmoid(gate) * val).astype(_BF16)
        gated_refs[slot_g][...] = gated
        return _slab_tokens(gated, zero_ref[...])

    def down(t, slot):
        c, r0 = item(t)
        acc_ref[pl.ds(r0, rs), :] += _dot(gated_refs[slot][...], wdown_ref[c])

    def stage(t, k, do_down=True, do_up=True):
        tokens = conv_gate(t, k % 4, k % 4)
        if do_up:
            up(t + 2, (k + 2) % 4, tokens)
        if do_down:
            down(t - 1, (k - 1) % 4)

    up(0, 0)
    up(1, 1)
    stage(0, 0, do_down=False)
    stage(1, 1)

    def four(q, carry):
        t = 4 * q + 2
        for k in range(4):
            stage(t + k, (2 + k) % 4)
        return carry

    lax.fori_loop(0, (n_items - 4) // 4, four, 0)
    stage(n_items - 2, (n_items - 2) % 4, do_up=False)
    stage(n_items - 1, (n_items - 1) % 4, do_up=False)
    down(n_items - 1, (n_items - 1) % 4)
    out_ref[0] = _layer_norm(alpha * x_ref[0] + acc_ref[...], g_ref[...], b_ref[...])


def _conv_ffn(h, wup_c, cw_c, cb_c, wdown_c, g, b, alpha):
    B, S, D = h.shape
    n2, _, fc = wup_c.shape
    n_chunks = n2 // 2
    tm = min(FFN_ROWS, S)
    rs = min(FFN_SLAB, tm)
    n_items = n_chunks * (tm // rs)
    assert n_items >= 8 and n_items % 4 == 0
    const2 = lambda b, s: (0, 0)
    const3 = lambda b, s: (0, 0, 0)
    return pl.pallas_call(
        functools.partial(_ffn_kernel, tm=tm, rs=rs, n_chunks=n_chunks, alpha=alpha),
        grid=(B, S // tm),
        in_specs=[
            pl.BlockSpec((1, tm, D), lambda b, s: (b, s, 0)),
            _resident((n2, D, fc), const3),
            _resident((n2, CONV_WIDTH, fc), const3),
            _resident((n2, 1, fc), const3),
            _resident((n_chunks, fc, D), const3),
            _resident((1, D), const2),
            _resident((1, D), const2),
            _resident((SUBLANES, LANES), const2),
        ],
        out_specs=pl.BlockSpec((1, tm, D), lambda b, s: (b, s, 0)),
        out_shape=jax.ShapeDtypeStruct((B, S, D), _F32),
        scratch_shapes=[
            pltpu.VMEM((tm, D), _F32),
            pltpu.VMEM((n2, SUBLANES, fc), _F32),
            pltpu.VMEM((tm, D), _BF16),
            pltpu.VMEM((2, rs, fc), _F32),
            pltpu.VMEM((2, rs, fc), _F32),
            pltpu.VMEM((2, rs, fc), _F32),
            pltpu.VMEM((2, rs, fc), _F32),
            pltpu.VMEM((rs, fc), _BF16),
            pltpu.VMEM((rs, fc), _BF16),
            pltpu.VMEM((rs, fc), _BF16),
            pltpu.VMEM((rs, fc), _BF16),
        ],
        compiler_params=pltpu.CompilerParams(
            dimension_semantics=("parallel", "arbitrary"), vmem_limit_bytes=VMEM_LIMIT_BYTES),
        name="conv_ffn",
    )(h, wup_c, cw_c, cb_c, wdown_c, g, b, jnp.zeros((SUBLANES, LANES), jnp.uint32))


GMLP_ROWS = 512
GMLP_SUBTILE = 256
FFN_ROWS = 1024
FFN_SLAB = 256
FFN_CHUNK = 256


def _chunk_cols(w, fc):
    r, c = w.shape
    return jnp.transpose(w.reshape(r, c // fc, fc), (1, 0, 2))


def kernel(x, attn_w_in, attn_w_out, gmlp_w_in, gmlp_ln_g, gmlp_ln_b, gmlp_w_s, gmlp_b_s, gmlp_w_out,
           ffn_w_up, ffn_conv_w, ffn_conv_b, ffn_w_down, ln_mix_g, ln_mix_b, ln_ffn_g, ln_ffn_b):
    B, S, D = x.shape
    depth = ffn_w_up.shape[0]
    alpha = (2 * depth) ** 0.25
    d_ff = ffn_w_down.shape[1]
    width = gmlp_w_out.shape[1]
    row = lambda v: v.reshape(1, -1)

    h = x
    for i in range(depth):
        j = i // 2
        if i % 2 == 0:
            w_in = attn_w_in[j]
            wq = w_in[:, :D].astype(_BF16)
            wk = w_in[:, D:2 * D].astype(_BF16)
            wvt = w_in[:, 2 * D:].T.astype(_BF16)
            q, kp, vt = _qkv_proj(h, wq, wk, wvt)
            o = _sb_attention(q, kp, vt)
            h = _mix_out_ln(o, h, attn_w_out[j].astype(_BF16), row(ln_mix_g[i]), row(ln_mix_b[i]), alpha)
        else:
            bs_full = jnp.repeat(gmlp_b_s[j].T, width // GMLP_GROUPS, axis=1)
            h = _gmlp_mixer(h, gmlp_w_in[j].astype(_BF16), row(gmlp_ln_g[j]), row(gmlp_ln_b[j]),
                            gmlp_w_s[j], bs_full, gmlp_w_out[j].astype(_BF16),
                            row(ln_mix_g[i]), row(ln_mix_b[i]), alpha)
        wup_c = _chunk_cols(ffn_w_up[i].astype(_BF16), FFN_CHUNK)
        cw_c = _chunk_cols(ffn_conv_w[i], FFN_CHUNK)
        cb_c = _chunk_cols(ffn_conv_b[i].reshape(1, -1), FFN_CHUNK)
        wdown_c = ffn_w_down[i].astype(_BF16).reshape(d_ff // FFN_CHUNK, FFN_CHUNK, D)
        h = _conv_ffn(h, wup_c, cw_c, cb_c, wdown_c, row(ln_ffn_g[i]), row(ln_ffn_b[i]), alpha)
    return h
```

```python
import functools
import math

import jax
import jax.numpy as jnp
from jax import lax
from jax.experimental import pallas as pl
from jax.experimental.pallas import tpu as pltpu

LN_EPS = 1e-5
CONV_WIDTH = 3
SB_HEAD_DIM = 64
GMLP_GROUPS = 8
GMLP_CHUNK = 128

SUBLANES = 8
LANES = 128
MXU_TILE = 256
KEY_BLOCK = 128
KEY_GROUP = KEY_BLOCK // SUBLANES
VMEM_LIMIT_BYTES = 56 * 1024 * 1024

ATT_TQ = 256
ATT_KB = 256
ATT_LANES = 1024
ATT_CHAIN = 4
LOG2E = 1.0 / math.log(2.0)
INV_LN2 = 1.0 / math.log(2.0)
LOG2_DEAD = 150.0

_BF16 = jnp.bfloat16
_F32 = jnp.float32


def _resident(block_shape, index_map):
    return pl.BlockSpec(block_shape, index_map, pipeline_mode=pl.Buffered(1))


def _layer_norm(y, g, b):
    mu = jnp.mean(y, axis=-1, keepdims=True)
    d = y - mu
    var = jnp.mean(d * d, axis=-1, keepdims=True)
    return d * lax.rsqrt(var + LN_EPS) * g + b


def _dot(a, b):
    return jnp.dot(a, b, preferred_element_type=_F32)


def _dot_nt(a, b):
    return lax.dot_general(a, b, (((1,), (1,)), ((), ())), preferred_element_type=_F32)


def _qkv_kernel(h_ref, wq_ref, wk_ref, wvt_ref, q_ref, kp_ref, vt_ref, *, tm, scale):
    hb = h_ref[0].astype(_BF16)
    row = lax.broadcasted_iota(jnp.int32, (KEY_BLOCK, KEY_BLOCK), 0)
    col = lax.broadcasted_iota(jnp.int32, (KEY_BLOCK, KEY_BLOCK), 1)
    perm = (col == (row % SUBLANES) * KEY_GROUP + row // SUBLANES).astype(_BF16)
    hp = jnp.concatenate(
        [_dot(perm, hb[blk * KEY_BLOCK:(blk + 1) * KEY_BLOCK, :]) for blk in range(tm // KEY_BLOCK)],
        axis=0).astype(_BF16)
    q_ref[0] = (_dot(hb, wq_ref[...]) * scale).astype(_BF16)
    kp_ref[0] = _dot(hp, wk_ref[...]).astype(_BF16)
    vt_ref[0] = _dot_nt(wvt_ref[...], hp).astype(_BF16)


def _qkv_proj(h, wq, wk, wvt):
    B, S, D = h.shape
    tm = min(512, S)
    scale = SB_HEAD_DIM ** -0.5 * LOG2E
    return pl.pallas_call(
        functools.partial(_qkv_kernel, tm=tm, scale=scale),
        grid=(B, S // tm),
        in_specs=[
            pl.BlockSpec((1, tm, D), lambda b, s: (b, s, 0)),
            _resident((D, D), lambda b, s: (0, 0)),
            _resident((D, D), lambda b, s: (0, 0)),
            _resident((D, D), lambda b, s: (0, 0)),
        ],
        out_specs=[
            pl.BlockSpec((1, tm, D), lambda b, s: (b, s, 0)),
            pl.BlockSpec((1, tm, D), lambda b, s: (b, s, 0)),
            pl.BlockSpec((1, D, tm), lambda b, s: (b, 0, s)),
        ],
        out_shape=[
            jax.ShapeDtypeStruct((B, S, D), _BF16),
            jax.ShapeDtypeStruct((B, S, D), _BF16),
            jax.ShapeDtypeStruct((B, D, S), _BF16),
        ],
        compiler_params=pltpu.CompilerParams(
            dimension_semantics=("parallel", "parallel"), vmem_limit_bytes=VMEM_LIMIT_BYTES),
        name="qkv_proj",
    )(h, wq, wk, wvt)


def _sublane_suffix_scan(g):
    row = lax.broadcasted_iota(jnp.int32, g.shape, 0)
    x = g
    for sh in (1, 2, 4):
        shifted = pltpu.roll(x, SUBLANES - sh, axis=0)
        x = x + jnp.where(row + sh < SUBLANES, shifted, 0.0)
    return x


def _sb_block(z, carry, causal):
    m = jnp.maximum(z, 0.0) + jnp.log(1.0 + jnp.exp2(-jnp.abs(z))) * INV_LN2
    if causal is not None:
        m = jnp.where(causal, m, 0.0)
    tiles = [m[SUBLANES * v:SUBLANES * (v + 1), :] for v in range(KEY_GROUP)]
    run = [None] * KEY_GROUP
    run[KEY_GROUP - 1] = tiles[KEY_GROUP - 1]
    for v in range(KEY_GROUP - 2, -1, -1):
        run[v] = run[v + 1] + tiles[v]
    group_tot = run[0]
    incl = _sublane_suffix_scan(group_tot)
    base = (incl - group_tot) + carry
    total = jnp.concatenate([run[v] + base for v in range(KEY_GROUP)], axis=0)
    a = jnp.exp2(z - total)
    if causal is not None:
        a = jnp.where(causal, a, 0.0)
    new_carry = carry + jnp.broadcast_to(incl[0:1, :], carry.shape)
    return a.astype(_BF16), new_carry


def _attn_kernel(q_ref, kp_ref, vt_ref, o_ref, acc_ref, carry_ref, *, tq, n_heads):
    qi = pl.program_id(2)
    lane = lax.broadcasted_iota(jnp.int32, (tq, LANES), 1)
    qm = []
    for h in range(n_heads):
        qpair = q_ref[0, :, LANES * (h // 2):LANES * (h // 2 + 1)]
        qm.append(jnp.where((lane // SB_HEAD_DIM) == h % 2, qpair, jnp.zeros_like(qpair)))

    acc_ref[...] = jnp.zeros_like(acc_ref)
    carry_ref[...] = jnp.zeros_like(carry_ref)

    def causal_mask(sub, q0):
        row = lax.broadcasted_iota(jnp.int32, (KEY_BLOCK, tq - q0), 0)
        key_off = (row % SUBLANES) * KEY_GROUP + row // SUBLANES + sub * KEY_BLOCK
        return key_off < lax.broadcasted_iota(jnp.int32, (KEY_BLOCK, tq - q0), 1) + q0

    def key_step(j, diagonal):
        n_sub = ATT_KB // KEY_BLOCK
        first_q = [sub * KEY_BLOCK if diagonal else 0 for sub in range(n_sub)]

        def scores(h):
            out = [None] * n_sub
            for sub in range(n_sub - 1, -1, -1):
                start = pl.multiple_of(j * ATT_KB + sub * KEY_BLOCK, KEY_BLOCK)
                kblk = kp_ref[0, pl.ds(start, KEY_BLOCK), LANES * (h // 2):LANES * (h // 2 + 1)]
                out[sub] = _dot_nt(kblk, qm[h][first_q[sub]:, :])
            return out

        z = {h: scores(h) for h in range(min(ATT_CHAIN, n_heads))}
        for h in range(n_heads):
            a_parts = [None] * n_sub
            for sub in range(n_sub - 1, -1, -1):
                q0 = first_q[sub]
                mask = causal_mask(sub, q0) if diagonal else None
                carry = carry_ref[h]
                a, new_carry = _sb_block(z[h][sub], carry[:, q0:], mask)
                if q0:
                    a = jnp.concatenate([jnp.zeros((KEY_BLOCK, q0), a.dtype), a], axis=1)
                    new_carry = jnp.concatenate([carry[:, :q0], new_carry], axis=1)
                a_parts[sub], carry_ref[h] = a, new_carry
            if h + ATT_CHAIN < n_heads:
                z[h + ATT_CHAIN] = scores(h + ATT_CHAIN)
            vt = vt_ref[0, SB_HEAD_DIM * h:SB_HEAD_DIM * (h + 1),
                        pl.ds(pl.multiple_of(j * ATT_KB, ATT_KB), ATT_KB)]
            acc_ref[SB_HEAD_DIM * h:SB_HEAD_DIM * (h + 1), :] += _dot(vt, jnp.concatenate(a_parts, axis=0))

    def any_alive():
        return jnp.min(carry_ref[...]) < LOG2_DEAD

    key_step(qi, True)

    def body(state):
        j, _ = state
        key_step(j, False)
        return j - 1, any_alive()

    lax.while_loop(lambda st: jnp.logical_and(st[0] >= 0, st[1]), body, (qi - 1, any_alive()))
    o_ref[0] = acc_ref[...].T.astype(o_ref.dtype)


def _sb_attention(q, kp, vt):
    B, S, D = q.shape
    tq = min(ATT_TQ, S)
    assert tq == ATT_KB and S % tq == 0 and D % ATT_LANES == 0
    n_heads = ATT_LANES // SB_HEAD_DIM
    return pl.pallas_call(
        functools.partial(_attn_kernel, tq=tq, n_heads=n_heads),
        grid=(B, D // ATT_LANES, S // tq),
        in_specs=[
            pl.BlockSpec((1, tq, ATT_LANES), lambda b, h, i: (b, i, h)),
            pl.BlockSpec((1, S, ATT_LANES), lambda b, h, i: (b, 0, h)),
            pl.BlockSpec((1, ATT_LANES, S), lambda b, h, i: (b, h, 0)),
        ],
        out_specs=pl.BlockSpec((1, tq, ATT_LANES), lambda b, h, i: (b, i, h)),
        out_shape=jax.ShapeDtypeStruct((B, S, D), _BF16),
        scratch_shapes=[
            pltpu.VMEM((ATT_LANES, tq), _F32),
            pltpu.VMEM((n_heads, SUBLANES, tq), _F32),
        ],
        compiler_params=pltpu.CompilerParams(
            dimension_semantics=("parallel", "parallel", "parallel"),
            vmem_limit_bytes=VMEM_LIMIT_BYTES),
        name="sb_attention",
    )(q, kp, vt)


def _mix_out_kernel(o_ref, h_ref, w_ref, g_ref, b_ref, out_ref, *, tm, rt, alpha):
    def proj(i):
        return _dot(o_ref[0, i * rt:(i + 1) * rt, :], w_ref[...])

    m = proj(0)
    for i in range(tm // rt):
        nxt = proj(i + 1) if (i + 1) * rt < tm else None
        rows = slice(i * rt, (i + 1) * rt)
        out_ref[0, rows, :] = _layer_norm(alpha * h_ref[0, rows, :] + m, g_ref[...], b_ref[...])
        m = nxt


def _mix_out_ln(o, h, w_out, g, b, alpha):
    B, S, D = h.shape
    tm = min(MIX_ROWS, S)
    rt = min(MIX_SUBTILE, tm)
    return pl.pallas_call(
        functools.partial(_mix_out_kernel, tm=tm, rt=rt, alpha=alpha),
        grid=(B, S // tm),
        in_specs=[
            pl.BlockSpec((1, tm, D), lambda b, s: (b, s, 0)),
            pl.BlockSpec((1, tm, D), lambda b, s: (b, s, 0)),
            _resident((D, D), lambda b, s: (0, 0)),
            _resident((1, D), lambda b, s: (0, 0)),
            _resident((1, D), lambda b, s: (0, 0)),
        ],
        out_specs=pl.BlockSpec((1, tm, D), lambda b, s: (b, s, 0)),
        out_shape=jax.ShapeDtypeStruct((B, S, D), _F32),
        compiler_params=pltpu.CompilerParams(
            dimension_semantics=("parallel", "parallel"), vmem_limit_bytes=VMEM_LIMIT_BYTES),
        name="mix_out_ln",
    )(o, h, w_out, g, b)


def _gelu_tanh(x):
    c = math.sqrt(2.0 / math.pi)
    return 0.5 * x * (1.0 + jnp.tanh(c * (x + 0.044715 * (x * x * x))))


def _gmlp_kernel(x_ref, win_ref, lng_ref, lnb_ref, ws_ref, bs_ref, wout_ref, g_ref, b_ref,
                 out_ref, *, tm, rt, width, alpha):
    gw = width // GMLP_GROUPS
    tri = (lax.broadcasted_iota(jnp.int32, (GMLP_CHUNK, GMLP_CHUNK), 0)
           >= lax.broadcasted_iota(jnp.int32, (GMLP_CHUNK, GMLP_CHUNK), 1))
    w_causal = [jnp.where(tri, ws_ref[g], 0.0).astype(_BF16) for g in range(GMLP_GROUPS)]

    def proj_in(i):
        xb = x_ref[0, i * rt:(i + 1) * rt, :].astype(_BF16)
        return _dot(xb, win_ref[:, :width]), _dot(xb, win_ref[:, width:])

    zz = proj_in(0)
    for i in range(tm // rt):
        nxt = proj_in(i + 1) if (i + 1) * rt < tm else None
        u = _gelu_tanh(zz[0])
        v = _gelu_tanh(zz[1])
        vn = _layer_norm(v, lng_ref[...], lnb_ref[...]).astype(_BF16)
        chunks = []
        for ch in range(rt // GMLP_CHUNK):
            r0 = ch * GMLP_CHUNK
            cols = []
            for g in range(GMLP_GROUPS):
                cols.append(_dot(w_causal[g], vn[r0:r0 + GMLP_CHUNK, g * gw:(g + 1) * gw]))
            chunks.append(jnp.concatenate(cols, axis=1) + bs_ref[...])
        s = jnp.concatenate(chunks, axis=0)
        m = _dot((u * s).astype(_BF16), wout_ref[...])
        rows = slice(i * rt, (i + 1) * rt)
        out_ref[0, rows, :] = _layer_norm(alpha * x_ref[0, rows, :] + m, g_ref[...], b_ref[...])
        zz = nxt


def _gmlp_mixer(h, w_in, ln_g, ln_b, w_s, bs_full, w_out, g, b, alpha):
    B, S, D = h.shape
    width = w_out.shape[0]
    tm = min(GMLP_ROWS, S)
    rt = min(GMLP_SUBTILE, tm)
    const2 = lambda b, s: (0, 0)
    return pl.pallas_call(
        functools.partial(_gmlp_kernel, tm=tm, rt=rt, width=width, alpha=alpha),
        grid=(B, S // tm),
        in_specs=[
            pl.BlockSpec((1, tm, D), lambda b, s: (b, s, 0)),
            _resident((D, 2 * width), const2),
            _resident((1, width), const2),
            _resident((1, width), const2),
            _resident((GMLP_GROUPS, GMLP_CHUNK, GMLP_CHUNK), lambda b, s: (0, 0, 0)),
            _resident((GMLP_CHUNK, width), const2),
            _resident((width, D), const2),
            _resident((1, D), const2),
            _resident((1, D), const2),
        ],
        out_specs=pl.BlockSpec((1, tm, D), lambda b, s: (b, s, 0)),
        out_shape=jax.ShapeDtypeStruct((B, S, D), _F32),
        compiler_params=pltpu.CompilerParams(
            dimension_semantics=("parallel", "parallel"), vmem_limit_bytes=VMEM_LIMIT_BYTES),
        name="gmlp_mixer",
    )(h, w_in, ln_g, ln_b, w_s, bs_full, w_out, g, b)


PACKED_ROWS = 2 * SUBLANES


def _slab_tokens(v, zero):
    u = pltpu.bitcast(v, jnp.uint32)
    tokens = []
    for j in range(u.shape[0] // SUBLANES):
        for l in range(u.shape[1] // LANES):
            tokens.append(u[SUBLANES * j:SUBLANES * (j + 1), LANES * l:LANES * (l + 1)] & zero)
    return tokens


def _tie_to_tokens(x, tokens, k_tile):
    n_slabs = x.shape[0] // PACKED_ROWS
    n_lanes = x.shape[1] // LANES
    pieces = [[x[PACKED_ROWS * i:PACKED_ROWS * (i + 1), LANES * l:LANES * (l + 1)] for l in range(n_lanes)]
              for i in range(n_slabs)]
    n_pos = (x.shape[1] // k_tile) * n_slabs
    for j, tok in enumerate(tokens):
        p = j * n_pos // len(tokens)
        k, i = p // n_slabs, p % n_slabs
        l = k * (k_tile // LANES)
        pieces[i][l] = pieces[i][l] + pltpu.bitcast(tok, _BF16)
    return jnp.concatenate([jnp.concatenate(row, axis=1) for row in pieces], axis=0)


def _ffn_kernel(x_ref, wup_ref, cw_ref, cb_ref, wdown_ref, g_ref, b_ref, zero_ref, out_ref,
                acc_ref, tail_ref, xb_ref, a0_ref, a1_ref, a2_ref, a3_ref,
                gated0_ref, gated1_ref, gated2_ref, gated3_ref, *, tm, rs, n_chunks, alpha):
    a_refs = (a0_ref, a1_ref, a2_ref, a3_ref)
    gated_refs = (gated0_ref, gated1_ref, gated2_ref, gated3_ref)
    n_slabs = tm // rs
    n_items = n_chunks * n_slabs

    @pl.when(pl.program_id(1) == 0)
    def _():
        tail_ref[...] = jnp.zeros_like(tail_ref)

    def cast_slab(r):
        xb_ref[r * rs:(r + 1) * rs, :] = x_ref[0, r * rs:(r + 1) * rs, :].astype(_BF16)

    def item(t):
        return t // n_slabs, pl.multiple_of((t % n_slabs) * rs, rs)

    def up(t, slot, tokens=()):
        c, r0 = item(t)
        xs = xb_ref[pl.ds(r0, rs), :]
        if tokens:
            xs = _tie_to_tokens(xs, tokens, MXU_TILE)
        for half in range(2):
            a_refs[slot][half] = _dot(xs, wup_ref[half * n_chunks + c])

    def conv_gate(t, slot_a, slot_g):
        c, _ = item(t)

        def conv_half(half):
            idx = half * n_chunks + c
            a = a_refs[slot_a][half]
            prev = tail_ref[idx]
            tail_ref[idx] = a[rs - SUBLANES:, :]
            ext = jnp.concatenate([prev, a], axis=0)
            a1 = pltpu.roll(ext, 1, axis=0)[SUBLANES:, :]
            a2 = pltpu.roll(ext, 2, axis=0)[SUBLANES:, :]
            w = cw_ref[idx]
            return cb_ref[idx] + w[0:1, :] * a2 + w[1:2, :] * a1 + w[2:3, :] * a

        gate = conv_half(0)
        val = conv_half(1)
        gated = (gate * jax.nn.sigmoid(gate) * val).astype(_BF16)
        gated_refs[slot_g][...] = gated
        return _slab_tokens(gated, zero_ref[...])

    def down(t, slot):
        c, r0 = item(t)
        acc_ref[pl.ds(r0, rs), :] += _dot(gated_refs[slot][...], wdown_ref[c])

    def stage(t, k, do_down=True, do_up=True):
        tokens = conv_gate(t, k % 4, k % 4)
        if do_up:
            up(t + 2, (k + 2) % 4, tokens)
        if do_down:
            down(t - 1, (k - 1) % 4)

    cast_slab(0)
    up(0, 0)
    if n_slabs > 1:
        cast_slab(1)
    up(1, 1)
    for r in range(2, n_slabs):
        cast_slab(r)
    acc_ref[...] = jnp.zeros_like(acc_ref)
    stage(0, 0, do_down=False)
    stage(1, 1)

    def four(q, carry):
        t = 4 * q + 2
        for k in range(4):
            stage(t + k, (2 + k) % 4)
        return carry

    lax.fori_loop(0, (n_items - 4) // 4, four, 0)
    stage(n_items - 2, (n_items - 2) % 4, do_up=False)
    stage(n_items - 1, (n_items - 1) % 4, do_up=False)
    down(n_items - 1, (n_items - 1) % 4)
    out_ref[0] = _layer_norm(alpha * x_ref[0] + acc_ref[...], g_ref[...], b_ref[...])


def _conv_ffn(h, wup_c, cw_c, cb_c, wdown_c, g, b, alpha):
    B, S, D = h.shape
    n2, _, fc = wup_c.shape
    n_chunks = n2 // 2
    tm = min(FFN_ROWS, S)
    rs = min(FFN_SLAB, tm)
    n_items = n_chunks * (tm // rs)
    assert n_items >= 8 and n_items % 4 == 0
    const2 = lambda b, s: (0, 0)
    const3 = lambda b, s: (0, 0, 0)
    return pl.pallas_call(
        functools.partial(_ffn_kernel, tm=tm, rs=rs, n_chunks=n_chunks, alpha=alpha),
        grid=(B, S // tm),
        in_specs=[
            pl.BlockSpec((1, tm, D), lambda b, s: (b, s, 0)),
            _resident((n2, D, fc), const3),
            _resident((n2, CONV_WIDTH, fc), const3),
            _resident((n2, 1, fc), const3),
            _resident((n_chunks, fc, D), const3),
            _resident((1, D), const2),
            _resident((1, D), const2),
            _resident((SUBLANES, LANES), const2),
        ],
        out_specs=pl.BlockSpec((1, tm, D), lambda b, s: (b, s, 0)),
        out_shape=jax.ShapeDtypeStruct((B, S, D), _F32),
        scratch_shapes=[
            pltpu.VMEM((tm, D), _F32),
            pltpu.VMEM((n2, SUBLANES, fc), _F32),
            pltpu.VMEM((tm, D), _BF16),
            pltpu.VMEM((2, rs, fc), _F32),
            pltpu.VMEM((2, rs, fc), _F32),
            pltpu.VMEM((2, rs, fc), _F32),
            pltpu.VMEM((2, rs, fc), _F32),
            pltpu.VMEM((rs, fc), _BF16),
            pltpu.VMEM((rs, fc), _BF16),
            pltpu.VMEM((rs, fc), _BF16),
            pltpu.VMEM((rs, fc), _BF16),
        ],
        compiler_params=pltpu.CompilerParams(
            dimension_semantics=("parallel", "arbitrary"), vmem_limit_bytes=VMEM_LIMIT_BYTES),
        name="conv_ffn",
    )(h, wup_c, cw_c, cb_c, wdown_c, g, b, jnp.zeros((SUBLANES, LANES), jnp.uint32))


MIX_ROWS = 1024
MIX_SUBTILE = 256
GMLP_ROWS = 512
GMLP_SUBTILE = 256
FFN_ROWS = 1024
FFN_SLAB = 256
FFN_CHUNK = 256


def _chunk_cols(w, fc):
    r, c = w.shape
    return jnp.transpose(w.reshape(r, c // fc, fc), (1, 0, 2))


def kernel(x, attn_w_in, attn_w_out, gmlp_w_in, gmlp_ln_g, gmlp_ln_b, gmlp_w_s, gmlp_b_s, gmlp_w_out,
           ffn_w_up, ffn_conv_w, ffn_conv_b, ffn_w_down, ln_mix_g, ln_mix_b, ln_ffn_g, ln_ffn_b):
    B, S, D = x.shape
    depth = ffn_w_up.shape[0]
    alpha = (2 * depth) ** 0.25
    d_ff = ffn_w_down.shape[1]
    width = gmlp_w_out.shape[1]
    row = lambda v: v.reshape(1, -1)

    h = x
    for i in range(depth):
        j = i // 2
        if i % 2 == 0:
            w_in = attn_w_in[j]
            wq = w_in[:, :D].astype(_BF16)
            wk = w_in[:, D:2 * D].astype(_BF16)
            wvt = w_in[:, 2 * D:].T.astype(_BF16)
            q, kp, vt = _qkv_proj(h, wq, wk, wvt)
            o = _sb_attention(q, kp, vt)
            h = _mix_out_ln(o, h, attn_w_out[j].astype(_BF16), row(ln_mix_g[i]), row(ln_mix_b[i]), alpha)
        else:
            bs_full = jnp.repeat(gmlp_b_s[j].T, width // GMLP_GROUPS, axis=1)
            h = _gmlp_mixer(h, gmlp_w_in[j].astype(_BF16), row(gmlp_ln_g[j]), row(gmlp_ln_b[j]),
                            gmlp_w_s[j], bs_full, gmlp_w_out[j].astype(_BF16),
                            row(ln_mix_g[i]), row(ln_mix_b[i]), alpha)
        wup_c = _chunk_cols(ffn_w_up[i].astype(_BF16), FFN_CHUNK)
        cw_c = _chunk_cols(ffn_conv_w[i], FFN_CHUNK)
        cb_c = _chunk_cols(ffn_conv_b[i].reshape(1, -1), FFN_CHUNK)
        wdown_c = ffn_w_down[i].astype(_BF16).reshape(d_ff // FFN_CHUNK, FFN_CHUNK, D)
        h = _conv_ffn(h, wup_c, cw_c, cb_c, wdown_c, row(ln_ffn_g[i]), row(ln_ffn_b[i]), alpha)
    return h
```

```python
import functools
import math

import jax
import jax.numpy as jnp
from jax import lax
from jax.experimental import pallas as pl
from jax.experimental.pallas import tpu as pltpu

LN_EPS = 1e-5
CONV_WIDTH = 3
SB_HEAD_DIM = 64
GMLP_GROUPS = 8
GMLP_CHUNK = 128

SUBLANES = 8
LANES = 128
MXU_TILE = 256
KEY_BLOCK = 128
KEY_GROUP = KEY_BLOCK // SUBLANES
VMEM_LIMIT_BYTES = 56 * 1024 * 1024

ATT_TQ = 256
ATT_KB = 256
ATT_LANES = 1024
ATT_CHAIN = 4
LOG2E = 1.0 / math.log(2.0)
INV_LN2 = 1.0 / math.log(2.0)
LOG2_DEAD = 150.0

_BF16 = jnp.bfloat16
_F32 = jnp.float32


def _resident(block_shape, index_map):
    return pl.BlockSpec(block_shape, index_map, pipeline_mode=pl.Buffered(1))


def _layer_norm(y, g, b):
    mu = jnp.mean(y, axis=-1, keepdims=True)
    d = y - mu
    var = jnp.mean(d * d, axis=-1, keepdims=True)
    return d * lax.rsqrt(var + LN_EPS) * g + b


def _dot(a, b):
    return jnp.dot(a, b, preferred_element_type=_F32)


def _dot_nt(a, b):
    return lax.dot_general(a, b, (((1,), (1,)), ((), ())), preferred_element_type=_F32)


def _qkv_kernel(h_ref, wq_ref, wk_ref, wvt_ref, q_ref, kp_ref, vt_ref, *, tm, scale):
    hb = h_ref[0].astype(_BF16)
    row = lax.broadcasted_iota(jnp.int32, (KEY_BLOCK, KEY_BLOCK), 0)
    col = lax.broadcasted_iota(jnp.int32, (KEY_BLOCK, KEY_BLOCK), 1)
    perm = (col == (row % SUBLANES) * KEY_GROUP + row // SUBLANES).astype(_BF16)
    hp = jnp.concatenate(
        [_dot(perm, hb[blk * KEY_BLOCK:(blk + 1) * KEY_BLOCK, :]) for blk in range(tm // KEY_BLOCK)],
        axis=0).astype(_BF16)
    q_ref[0] = (_dot(hb, wq_ref[...]) * scale).astype(_BF16)
    kp_ref[0] = _dot(hp, wk_ref[...]).astype(_BF16)
    vt_ref[0] = _dot_nt(wvt_ref[...], hp).astype(_BF16)


def _qkv_proj(h, wq, wk, wvt):
    B, S, D = h.shape
    tm = min(512, S)
    scale = SB_HEAD_DIM ** -0.5 * LOG2E
    return pl.pallas_call(
        functools.partial(_qkv_kernel, tm=tm, scale=scale),
        grid=(B, S // tm),
        in_specs=[
            pl.BlockSpec((1, tm, D), lambda b, s: (b, s, 0)),
            _resident((D, D), lambda b, s: (0, 0)),
            _resident((D, D), lambda b, s: (0, 0)),
            _resident((D, D), lambda b, s: (0, 0)),
        ],
        out_specs=[
            pl.BlockSpec((1, tm, D), lambda b, s: (b, s, 0)),
            pl.BlockSpec((1, tm, D), lambda b, s: (b, s, 0)),
            pl.BlockSpec((1, D, tm), lambda b, s: (b, 0, s)),
        ],
        out_shape=[
            jax.ShapeDtypeStruct((B, S, D), _BF16),
            jax.ShapeDtypeStruct((B, S, D), _BF16),
            jax.ShapeDtypeStruct((B, D, S), _BF16),
        ],
        compiler_params=pltpu.CompilerParams(
            dimension_semantics=("parallel", "parallel"), vmem_limit_bytes=VMEM_LIMIT_BYTES),
        name="qkv_proj",
    )(h, wq, wk, wvt)


def _sublane_suffix_scan(g):
    row = lax.broadcasted_iota(jnp.int32, g.shape, 0)
    x = g
    for sh in (1, 2, 4):
        shifted = pltpu.roll(x, SUBLANES - sh, axis=0)
        x = x + jnp.where(row + sh < SUBLANES, shifted, 0.0)
    return x


def _sb_block(z, carry, causal):
    m = jnp.maximum(z, 0.0) + jnp.log(1.0 + jnp.exp2(-jnp.abs(z))) * INV_LN2
    if causal is not None:
        m = jnp.where(causal, m, 0.0)
    tiles = [m[SUBLANES * v:SUBLANES * (v + 1), :] for v in range(KEY_GROUP)]
    run = [None] * KEY_GROUP
    run[KEY_GROUP - 1] = tiles[KEY_GROUP - 1]
    for v in range(KEY_GROUP - 2, -1, -1):
        run[v] = run[v + 1] + tiles[v]
    group_tot = run[0]
    incl = _sublane_suffix_scan(group_tot)
    base = (incl - group_tot) + carry
    total = jnp.concatenate([run[v] + base for v in range(KEY_GROUP)], axis=0)
    a = jnp.exp2(z - total)
    if causal is not None:
        a = jnp.where(causal, a, 0.0)
    new_carry = carry + jnp.broadcast_to(incl[0:1, :], carry.shape)
    return a.astype(_BF16), new_carry


def _attn_kernel(q_ref, kp_ref, vt_ref, o_ref, acc_ref, carry_ref, *, tq, n_heads):
    qi = pl.program_id(2)
    lane = lax.broadcasted_iota(jnp.int32, (tq, LANES), 1)
    qm = []
    for h in range(n_heads):
        qpair = q_ref[0, :, LANES * (h // 2):LANES * (h // 2 + 1)]
        qm.append(jnp.where((lane // SB_HEAD_DIM) == h % 2, qpair, jnp.zeros_like(qpair)))

    acc_ref[...] = jnp.zeros_like(acc_ref)
    carry_ref[...] = jnp.zeros_like(carry_ref)

    def causal_mask(sub, q0):
        row = lax.broadcasted_iota(jnp.int32, (KEY_BLOCK, tq - q0), 0)
        key_off = (row % SUBLANES) * KEY_GROUP + row // SUBLANES + sub * KEY_BLOCK
        return key_off < lax.broadcasted_iota(jnp.int32, (KEY_BLOCK, tq - q0), 1) + q0

    def key_step(j, diagonal):
        n_sub = ATT_KB // KEY_BLOCK
        first_q = [sub * KEY_BLOCK if diagonal else 0 for sub in range(n_sub)]

        def scores(h):
            out = [None] * n_sub
            for sub in range(n_sub - 1, -1, -1):
                start = pl.multiple_of(j * ATT_KB + sub * KEY_BLOCK, KEY_BLOCK)
                kblk = kp_ref[0, pl.ds(start, KEY_BLOCK), LANES * (h // 2):LANES * (h // 2 + 1)]
                out[sub] = _dot_nt(kblk, qm[h][first_q[sub]:, :])
            return out

        z = {h: scores(h) for h in range(min(ATT_CHAIN, n_heads))}
        for h in range(n_heads):
            a_parts = [None] * n_sub
            for sub in range(n_sub - 1, -1, -1):
                q0 = first_q[sub]
                mask = causal_mask(sub, q0) if diagonal else None
                carry = carry_ref[h]
                a, new_carry = _sb_block(z[h][sub], carry[:, q0:], mask)
                if q0:
                    a = jnp.concatenate([jnp.zeros((KEY_BLOCK, q0), a.dtype), a], axis=1)
                    new_carry = jnp.concatenate([carry[:, :q0], new_carry], axis=1)
                a_parts[sub], carry_ref[h] = a, new_carry
            if h + ATT_CHAIN < n_heads:
                z[h + ATT_CHAIN] = scores(h + ATT_CHAIN)
            vt = vt_ref[0, SB_HEAD_DIM * h:SB_HEAD_DIM * (h + 1),
                        pl.ds(pl.multiple_of(j * ATT_KB, ATT_KB), ATT_KB)]
            acc_ref[SB_HEAD_DIM * h:SB_HEAD_DIM * (h + 1), :] += _dot(vt, jnp.concatenate(a_parts, axis=0))

    def any_alive():
        return jnp.min(carry_ref[...]) < LOG2_DEAD

    key_step(qi, True)

    def body(state):
        j, _ = state
        key_step(j, False)
        return j - 1, any_alive()

    lax.while_loop(lambda st: jnp.logical_and(st[0] >= 0, st[1]), body, (qi - 1, any_alive()))
    o_ref[0] = acc_ref[...].T.astype(o_ref.dtype)


def _sb_attention(q, kp, vt):
    B, S, D = q.shape
    tq = min(ATT_TQ, S)
    assert tq == ATT_KB and S % tq == 0 and D % ATT_LANES == 0
    n_heads = ATT_LANES // SB_HEAD_DIM
    return pl.pallas_call(
        functools.partial(_attn_kernel, tq=tq, n_heads=n_heads),
        grid=(B, D // ATT_LANES, S // tq),
        in_specs=[
            pl.BlockSpec((1, tq, ATT_LANES), lambda b, h, i: (b, i, h)),
            pl.BlockSpec((1, S, ATT_LANES), lambda b, h, i: (b, 0, h)),
            pl.BlockSpec((1, ATT_LANES, S), lambda b, h, i: (b, h, 0)),
        ],
        out_specs=pl.BlockSpec((1, tq, ATT_LANES), lambda b, h, i: (b, i, h)),
        out_shape=jax.ShapeDtypeStruct((B, S, D), _BF16),
        scratch_shapes=[
            pltpu.VMEM((ATT_LANES, tq), _F32),
            pltpu.VMEM((n_heads, SUBLANES, tq), _F32),
        ],
        compiler_params=pltpu.CompilerParams(
            dimension_semantics=("parallel", "parallel", "parallel"),
            vmem_limit_bytes=VMEM_LIMIT_BYTES),
        name="sb_attention",
    )(q, kp, vt)


def _mix_out_kernel(o_ref, h_ref, w_ref, g_ref, b_ref, out_ref, *, tm, rt, alpha):
    def proj(i):
        return _dot(o_ref[0, i * rt:(i + 1) * rt, :], w_ref[...])

    m = proj(0)
    for i in range(tm // rt):
        nxt = proj(i + 1) if (i + 1) * rt < tm else None
        rows = slice(i * rt, (i + 1) * rt)
        out_ref[0, rows, :] = _layer_norm(alpha * h_ref[0, rows, :] + m, g_ref[...], b_ref[...])
        m = nxt


def _mix_out_ln(o, h, w_out, g, b, alpha):
    B, S, D = h.shape
    tm = min(MIX_ROWS, S)
    rt = min(MIX_SUBTILE, tm)
    return pl.pallas_call(
        functools.partial(_mix_out_kernel, tm=tm, rt=rt, alpha=alpha),
        grid=(B, S // tm),
        in_specs=[
            pl.BlockSpec((1, tm, D), lambda b, s: (b, s, 0)),
            pl.BlockSpec((1, tm, D), lambda b, s: (b, s, 0)),
            _resident((D, D), lambda b, s: (0, 0)),
            _resident((1, D), lambda b, s: (0, 0)),
            _resident((1, D), lambda b, s: (0, 0)),
        ],
        out_specs=pl.BlockSpec((1, tm, D), lambda b, s: (b, s, 0)),
        out_shape=jax.ShapeDtypeStruct((B, S, D), _F32),
        compiler_params=pltpu.CompilerParams(
            dimension_semantics=("parallel", "parallel"), vmem_limit_bytes=VMEM_LIMIT_BYTES),
        name="mix_out_ln",
    )(o, h, w_out, g, b)


def _gelu_tanh(x):
    c = math.sqrt(2.0 / math.pi)
    return 0.5 * x * (1.0 + jnp.tanh(c * (x + 0.044715 * (x * x * x))))


def _gmlp_kernel(x_ref, win_ref, lng_ref, lnb_ref, ws_ref, bs_ref, wout_ref, g_ref, b_ref,
                 out_ref, *, tm, rt, width, alpha):
    gw = width // GMLP_GROUPS
    tri = (lax.broadcasted_iota(jnp.int32, (GMLP_CHUNK, GMLP_CHUNK), 0)
           >= lax.broadcasted_iota(jnp.int32, (GMLP_CHUNK, GMLP_CHUNK), 1))
    w_causal = [jnp.where(tri, ws_ref[g], 0.0).astype(_BF16) for g in range(GMLP_GROUPS)]

    def proj_in(i):
        xb = x_ref[0, i * rt:(i + 1) * rt, :].astype(_BF16)
        return _dot(xb, win_ref[:, :width]), _dot(xb, win_ref[:, width:])

    zz = proj_in(0)
    for i in range(tm // rt):
        nxt = proj_in(i + 1) if (i + 1) * rt < tm else None
        u = _gelu_tanh(zz[0])
        v = _gelu_tanh(zz[1])
        vn = _layer_norm(v, lng_ref[...], lnb_ref[...]).astype(_BF16)
        chunks = []
        for ch in range(rt // GMLP_CHUNK):
            r0 = ch * GMLP_CHUNK
            cols = []
            for g in range(GMLP_GROUPS):
                cols.append(_dot(w_causal[g], vn[r0:r0 + GMLP_CHUNK, g * gw:(g + 1) * gw]))
            chunks.append(jnp.concatenate(cols, axis=1) + bs_ref[...])
        s = jnp.concatenate(chunks, axis=0)
        m = _dot((u * s).astype(_BF16), wout_ref[...])
        rows = slice(i * rt, (i + 1) * rt)
        out_ref[0, rows, :] = _layer_norm(alpha * x_ref[0, rows, :] + m, g_ref[...], b_ref[...])
        zz = nxt


def _gmlp_mixer(h, w_in, ln_g, ln_b, w_s, bs_full, w_out, g, b, alpha):
    B, S, D = h.shape
    width = w_out.shape[0]
    tm = min(GMLP_ROWS, S)
    rt = min(GMLP_SUBTILE, tm)
    const2 = lambda b, s: (0, 0)
    return pl.pallas_call(
        functools.partial(_gmlp_kernel, tm=tm, rt=rt, width=width, alpha=alpha),
        grid=(B, S // tm),
        in_specs=[
            pl.BlockSpec((1, tm, D), lambda b, s: (b, s, 0)),
            _resident((D, 2 * width), const2),
            _resident((1, width), const2),
            _resident((1, width), const2),
            _resident((GMLP_GROUPS, GMLP_CHUNK, GMLP_CHUNK), lambda b, s: (0, 0, 0)),
            _resident((GMLP_CHUNK, width), const2),
            _resident((width, D), const2),
            _resident((1, D), const2),
            _resident((1, D), const2),
        ],
        out_specs=pl.BlockSpec((1, tm, D), lambda b, s: (b, s, 0)),
        out_shape=jax.ShapeDtypeStruct((B, S, D), _F32),
        compiler_params=pltpu.CompilerParams(
            dimension_semantics=("parallel", "parallel"), vmem_limit_bytes=VMEM_LIMIT_BYTES),
        name="gmlp_mixer",
    )(h, w_in, ln_g, ln_b, w_s, bs_full, w_out, g, b)


PACKED_ROWS = 2 * SUBLANES


def _slab_tokens(v, zero):
    u = pltpu.bitcast(v, jnp.uint32)
    tokens = []
    for j in range(u.shape[0] // SUBLANES):
        for l in range(u.shape[1] // LANES):
            tokens.append(u[SUBLANES * j:SUBLANES * (j + 1), LANES * l:LANES * (l + 1)] & zero)
    return tokens


def _tie_to_tokens(x, tokens, k_tile):
    n_slabs = x.shape[0] // PACKED_ROWS
    n_lanes = x.shape[1] // LANES
    pieces = [[x[PACKED_ROWS * i:PACKED_ROWS * (i + 1), LANES * l:LANES * (l + 1)] for l in range(n_lanes)]
              for i in range(n_slabs)]
    n_pos = (x.shape[1] // k_tile) * n_slabs
    for j, tok in enumerate(tokens):
        p = j * n_pos // len(tokens)
        k, i = p // n_slabs, p % n_slabs
        l = k * (k_tile // LANES)
        pieces[i][l] = pieces[i][l] + pltpu.bitcast(tok, _BF16)
    return jnp.concatenate([jnp.concatenate(row, axis=1) for row in pieces], axis=0)


def _ffn_kernel(x_ref, wup_ref, cw_ref, cb_ref, wdown_ref, g_ref, b_ref, zero_ref, out_ref,
                acc_ref, tail_ref, xb_ref, a0_ref, a1_ref, a2_ref, a3_ref,
                gated0_ref, gated1_ref, gated2_ref, gated3_ref, *, tm, rs, fc, n_chunks, alpha):
    a_refs = (a0_ref, a1_ref, a2_ref, a3_ref)
    gated_refs = (gated0_ref, gated1_ref, gated2_ref, gated3_ref)
    n_slabs = tm // rs
    n_items = n_chunks * n_slabs

    @pl.when(pl.program_id(1) == 0)
    def _():
        tail_ref[...] = jnp.zeros_like(tail_ref)

    def cast_slab(r):
        xb_ref[r * rs:(r + 1) * rs, :] = x_ref[0, r * rs:(r + 1) * rs, :].astype(_BF16)

    def item(t):
        return t // n_slabs, pl.multiple_of((t % n_slabs) * rs, rs)

    def cols(half, c):
        return pl.ds(pl.multiple_of((half * n_chunks + c) * fc, fc), fc)

    def up(t, slot, tokens=()):
        c, r0 = item(t)
        xs = xb_ref[pl.ds(r0, rs), :]
        if tokens:
            xs = _tie_to_tokens(xs, tokens, MXU_TILE)
        for half in range(2):
            a_refs[slot][half] = _dot(xs, wup_ref[:, cols(half, c)])

    def conv_gate(t, slot_a, slot_g):
        c, _ = item(t)

        def conv_half(half):
            idx = half * n_chunks + c
            a = a_refs[slot_a][half]
            prev = tail_ref[idx]
            tail_ref[idx] = a[rs - SUBLANES:, :]
            ext = jnp.concatenate([prev, a], axis=0)
            a1 = pltpu.roll(ext, 1, axis=0)[SUBLANES:, :]
            a2 = pltpu.roll(ext, 2, axis=0)[SUBLANES:, :]
            w = cw_ref[:, cols(half, c)]
            return cb_ref[:, cols(half, c)] + w[0:1, :] * a2 + w[1:2, :] * a1 + w[2:3, :] * a

        gate = conv_half(0)
        val = conv_half(1)
        gated = (gate * jax.nn.sigmoid(gate) * val).astype(_BF16)
        gated_refs[slot_g][...] = gated
        return _slab_tokens(gated, zero_ref[...])

    def down(t, slot):
        c, r0 = item(t)
        w_rows = pl.ds(pl.multiple_of(c * fc, fc), fc)
        acc_ref[pl.ds(r0, rs), :] += _dot(gated_refs[slot][...], wdown_ref[w_rows, :])

    def stage(t, k, do_down=True, do_up=True):
        tokens = conv_gate(t, k % 4, k % 4)
        if do_up:
            up(t + 2, (k + 2) % 4, tokens)
        if do_down:
            down(t - 1, (k - 1) % 4)

    cast_slab(0)
    up(0, 0)
    if n_slabs > 1:
        cast_slab(1)
    up(1, 1)
    for r in range(2, n_slabs):
        cast_slab(r)
    acc_ref[...] = jnp.zeros_like(acc_ref)
    stage(0, 0, do_down=False)
    stage(1, 1)

    def four(q, carry):
        t = 4 * q + 2
        for k in range(4):
            stage(t + k, (2 + k) % 4)
        return carry

    lax.fori_loop(0, (n_items - 4) // 4, four, 0)
    stage(n_items - 2, (n_items - 2) % 4, do_up=False)
    stage(n_items - 1, (n_items - 1) % 4, do_up=False)
    down(n_items - 1, (n_items - 1) % 4)
    out_ref[0] = _layer_norm(alpha * x_ref[0] + acc_ref[...], g_ref[...], b_ref[...])


def _conv_ffn(h, w_up, conv_w, conv_b, w_down, g, b, alpha):
    B, S, D = h.shape
    d_ff = w_down.shape[0]
    fc = FFN_CHUNK
    n_chunks = d_ff // fc
    n2 = 2 * n_chunks
    tm = min(FFN_ROWS, S)
    rs = min(FFN_SLAB, tm)
    n_items = n_chunks * (tm // rs)
    assert n_items >= 8 and n_items % 4 == 0
    const2 = lambda b, s: (0, 0)
    return pl.pallas_call(
        functools.partial(_ffn_kernel, tm=tm, rs=rs, fc=fc, n_chunks=n_chunks, alpha=alpha),
        grid=(B, S // tm),
        in_specs=[
            pl.BlockSpec((1, tm, D), lambda b, s: (b, s, 0)),
            _resident((D, 2 * d_ff), const2),
            _resident((CONV_WIDTH, 2 * d_ff), const2),
            _resident((1, 2 * d_ff), const2),
            _resident((d_ff, D), const2),
            _resident((1, D), const2),
            _resident((1, D), const2),
            _resident((SUBLANES, LANES), const2),
        ],
        out_specs=pl.BlockSpec((1, tm, D), lambda b, s: (b, s, 0)),
        out_shape=jax.ShapeDtypeStruct((B, S, D), _F32),
        scratch_shapes=[
            pltpu.VMEM((tm, D), _F32),
            pltpu.VMEM((n2, SUBLANES, fc), _F32),
            pltpu.VMEM((tm, D), _BF16),
            pltpu.VMEM((2, rs, fc), _F32),
            pltpu.VMEM((2, rs, fc), _F32),
            pltpu.VMEM((2, rs, fc), _F32),
            pltpu.VMEM((2, rs, fc), _F32),
            pltpu.VMEM((rs, fc), _BF16),
            pltpu.VMEM((rs, fc), _BF16),
            pltpu.VMEM((rs, fc), _BF16),
            pltpu.VMEM((rs, fc), _BF16),
        ],
        compiler_params=pltpu.CompilerParams(
            dimension_semantics=("parallel", "arbitrary"), vmem_limit_bytes=VMEM_LIMIT_BYTES),
        name="conv_ffn",
    )(h, w_up, conv_w, conv_b, w_down, g, b, jnp.zeros((SUBLANES, LANES), jnp.uint32))


MIX_ROWS = 1024
MIX_SUBTILE = 256
GMLP_ROWS = 512
GMLP_SUBTILE = 256
FFN_ROWS = 1024
FFN_SLAB = 256
FFN_CHUNK = 256


def kernel(x, attn_w_in, attn_w_out, gmlp_w_in, gmlp_ln_g, gmlp_ln_b, gmlp_w_s, gmlp_b_s, gmlp_w_out,
           ffn_w_up, ffn_conv_w, ffn_conv_b, ffn_w_down, ln_mix_g, ln_mix_b, ln_ffn_g, ln_ffn_b):
    B, S, D = x.shape
    depth = ffn_w_up.shape[0]
    alpha = (2 * depth) ** 0.25
    width = gmlp_w_out.shape[1]
    row = lambda v: v.reshape(1, -1)

    h = x
    for i in range(depth):
        j = i // 2
        if i % 2 == 0:
            w_in = attn_w_in[j]
            wq = w_in[:, :D].astype(_BF16)
            wk = w_in[:, D:2 * D].astype(_BF16)
            wvt = w_in[:, 2 * D:].T.astype(_BF16)
            q, kp, vt = _qkv_proj(h, wq, wk, wvt)
            o = _sb_attention(q, kp, vt)
            h = _mix_out_ln(o, h, attn_w_out[j].astype(_BF16), row(ln_mix_g[i]), row(ln_mix_b[i]), alpha)
        else:
            bs_full = jnp.repeat(gmlp_b_s[j].T, width // GMLP_GROUPS, axis=1)
            h = _gmlp_mixer(h, gmlp_w_in[j].astype(_BF16), row(gmlp_ln_g[j]), row(gmlp_ln_b[j]),
                            gmlp_w_s[j], bs_full, gmlp_w_out[j].astype(_BF16),
                            row(ln_mix_g[i]), row(ln_mix_b[i]), alpha)
        h = _conv_ffn(h, ffn_w_up[i].astype(_BF16), ffn_conv_w[i], row(ffn_conv_b[i]),
                      ffn_w_down[i].astype(_BF16), row(ln_ffn_g[i]), row(ln_ffn_b[i]), alpha)
    return h
```

```python
import functools
import math

import jax
import jax.numpy as jnp
from jax import lax
from jax.experimental import pallas as pl
from jax.experimental.pallas import tpu as pltpu

LN_EPS = 1e-5
CONV_WIDTH = 3
SB_HEAD_DIM = 64
GMLP_GROUPS = 8
GMLP_CHUNK = 128

SUBLANES = 8
LANES = 128
MXU_TILE = 256
KEY_BLOCK = 128
KEY_GROUP = KEY_BLOCK // SUBLANES
VMEM_LIMIT_BYTES = 56 * 1024 * 1024

ATT_TQ = 256
ATT_KB = 256
ATT_LANES = 1024
ATT_CHAIN = 4
LOG2E = 1.0 / math.log(2.0)
INV_LN2 = 1.0 / math.log(2.0)
LOG2_DEAD = 150.0

_BF16 = jnp.bfloat16
_F32 = jnp.float32


def _resident(block_shape, index_map):
    return pl.BlockSpec(block_shape, index_map, pipeline_mode=pl.Buffered(1))


def _layer_norm(y, g, b):
    mu = jnp.mean(y, axis=-1, keepdims=True)
    d = y - mu
    var = jnp.mean(d * d, axis=-1, keepdims=True)
    return d * lax.rsqrt(var + LN_EPS) * g + b


def _dot(a, b):
    return jnp.dot(a, b, preferred_element_type=_F32)


def _dot_nt(a, b):
    return lax.dot_general(a, b, (((1,), (1,)), ((), ())), preferred_element_type=_F32)


def _qkv_kernel(h_ref, wq_ref, wk_ref, wvt_ref, q_ref, kp_ref, vt_ref, *, tm, scale):
    hb = h_ref[0].astype(_BF16)
    row = lax.broadcasted_iota(jnp.int32, (KEY_BLOCK, KEY_BLOCK), 0)
    col = lax.broadcasted_iota(jnp.int32, (KEY_BLOCK, KEY_BLOCK), 1)
    perm = (col == (row % SUBLANES) * KEY_GROUP + row // SUBLANES).astype(_BF16)
    hp = jnp.concatenate(
        [_dot(perm, hb[blk * KEY_BLOCK:(blk + 1) * KEY_BLOCK, :]) for blk in range(tm // KEY_BLOCK)],
        axis=0).astype(_BF16)
    q_ref[0] = (_dot(hb, wq_ref[...]) * scale).astype(_BF16)
    kp_ref[0] = _dot(hp, wk_ref[...]).astype(_BF16)
    vt_ref[0] = _dot_nt(wvt_ref[...], hp).astype(_BF16)


def _qkv_proj(h, wq, wk, wvt):
    B, S, D = h.shape
    tm = min(512, S)
    scale = SB_HEAD_DIM ** -0.5 * LOG2E
    return pl.pallas_call(
        functools.partial(_qkv_kernel, tm=tm, scale=scale),
        grid=(B, S // tm),
        in_specs=[
            pl.BlockSpec((1, tm, D), lambda b, s: (b, s, 0)),
            _resident((D, D), lambda b, s: (0, 0)),
            _resident((D, D), lambda b, s: (0, 0)),
            _resident((D, D), lambda b, s: (0, 0)),
        ],
        out_specs=[
            pl.BlockSpec((1, tm, D), lambda b, s: (b, s, 0)),
            pl.BlockSpec((1, tm, D), lambda b, s: (b, s, 0)),
            pl.BlockSpec((1, D, tm), lambda b, s: (b, 0, s)),
        ],
        out_shape=[
            jax.ShapeDtypeStruct((B, S, D), _BF16),
            jax.ShapeDtypeStruct((B, S, D), _BF16),
            jax.ShapeDtypeStruct((B, D, S), _BF16),
        ],
        compiler_params=pltpu.CompilerParams(
            dimension_semantics=("parallel", "parallel"), vmem_limit_bytes=VMEM_LIMIT_BYTES),
        name="qkv_proj",
    )(h, wq, wk, wvt)


def _sublane_suffix_scan(g):
    row = lax.broadcasted_iota(jnp.int32, g.shape, 0)
    x = g
    for sh in (1, 2, 4):
        shifted = pltpu.roll(x, SUBLANES - sh, axis=0)
        x = x + jnp.where(row + sh < SUBLANES, shifted, 0.0)
    return x


def _sb_block(z, carry, causal):
    m = jnp.maximum(z, 0.0) + jnp.log(1.0 + jnp.exp2(-jnp.abs(z))) * INV_LN2
    if causal is not None:
        m = jnp.where(causal, m, 0.0)
    tiles = [m[SUBLANES * v:SUBLANES * (v + 1), :] for v in range(KEY_GROUP)]
    run = [None] * KEY_GROUP
    run[KEY_GROUP - 1] = tiles[KEY_GROUP - 1]
    for v in range(KEY_GROUP - 2, -1, -1):
        run[v] = run[v + 1] + tiles[v]
    group_tot = run[0]
    incl = _sublane_suffix_scan(group_tot)
    base = (incl - group_tot) + carry
    total = jnp.concatenate([run[v] + base for v in range(KEY_GROUP)], axis=0)
    a = jnp.exp2(z - total)
    if causal is not None:
        a = jnp.where(causal, a, 0.0)
    new_carry = carry + jnp.broadcast_to(incl[0:1, :], carry.shape)
    return a.astype(_BF16), new_carry


def _attn_kernel(q_ref, kp_ref, vt_ref, o_ref, acc_ref, carry_ref, *, tq, n_heads):
    qi = pl.program_id(2)
    lane = lax.broadcasted_iota(jnp.int32, (tq, LANES), 1)
    qm = []
    for h in range(n_heads):
        qpair = q_ref[0, :, LANES * (h // 2):LANES * (h // 2 + 1)]
        qm.append(jnp.where((lane // SB_HEAD_DIM) == h % 2, qpair, jnp.zeros_like(qpair)))

    acc_ref[...] = jnp.zeros_like(acc_ref)
    carry_ref[...] = jnp.zeros_like(carry_ref)

    def causal_mask(sub, q0):
        row = lax.broadcasted_iota(jnp.int32, (KEY_BLOCK, tq - q0), 0)
        key_off = (row % SUBLANES) * KEY_GROUP + row // SUBLANES + sub * KEY_BLOCK
        return key_off < lax.broadcasted_iota(jnp.int32, (KEY_BLOCK, tq - q0), 1) + q0

    def key_step(j, diagonal):
        n_sub = ATT_KB // KEY_BLOCK
        first_q = [sub * KEY_BLOCK if diagonal else 0 for sub in range(n_sub)]

        def scores(h):
            out = [None] * n_sub
            for sub in range(n_sub - 1, -1, -1):
                start = pl.multiple_of(j * ATT_KB + sub * KEY_BLOCK, KEY_BLOCK)
                kblk = kp_ref[0, pl.ds(start, KEY_BLOCK), LANES * (h // 2):LANES * (h // 2 + 1)]
                out[sub] = _dot_nt(kblk, qm[h][first_q[sub]:, :])
            return out

        z = {h: scores(h) for h in range(min(ATT_CHAIN, n_heads))}
        for h in range(n_heads):
            a_parts = [None] * n_sub
            for sub in range(n_sub - 1, -1, -1):
                q0 = first_q[sub]
                mask = causal_mask(sub, q0) if diagonal else None
                carry = carry_ref[h]
                a, new_carry = _sb_block(z[h][sub], carry[:, q0:], mask)
                if q0:
                    a = jnp.concatenate([jnp.zeros((KEY_BLOCK, q0), a.dtype), a], axis=1)
                    new_carry = jnp.concatenate([carry[:, :q0], new_carry], axis=1)
                a_parts[sub], carry_ref[h] = a, new_carry
            if h + ATT_CHAIN < n_heads:
                z[h + ATT_CHAIN] = scores(h + ATT_CHAIN)
            vt = vt_ref[0, SB_HEAD_DIM * h:SB_HEAD_DIM * (h + 1),
                        pl.ds(pl.multiple_of(j * ATT_KB, ATT_KB), ATT_KB)]
            acc_ref[SB_HEAD_DIM * h:SB_HEAD_DIM * (h + 1), :] += _dot(vt, jnp.concatenate(a_parts, axis=0))

    def any_alive():
        return jnp.min(carry_ref[...]) < LOG2_DEAD

    key_step(qi, True)

    def body(state):
        j, _ = state
        key_step(j, False)
        return j - 1, any_alive()

    lax.while_loop(lambda st: jnp.logical_and(st[0] >= 0, st[1]), body, (qi - 1, any_alive()))
    o_ref[0] = acc_ref[...].T.astype(o_ref.dtype)


def _sb_attention(q, kp, vt):
    B, S, D = q.shape
    tq = min(ATT_TQ, S)
    assert tq == ATT_KB and S % tq == 0 and D % ATT_LANES == 0
    n_heads = ATT_LANES // SB_HEAD_DIM
    return pl.pallas_call(
        functools.partial(_attn_kernel, tq=tq, n_heads=n_heads),
        grid=(B, D // ATT_LANES, S // tq),
        in_specs=[
            pl.BlockSpec((1, tq, ATT_LANES), lambda b, h, i: (b, i, h)),
            pl.BlockSpec((1, S, ATT_LANES), lambda b, h, i: (b, 0, h)),
            pl.BlockSpec((1, ATT_LANES, S), lambda b, h, i: (b, h, 0)),
        ],
        out_specs=pl.BlockSpec((1, tq, ATT_LANES), lambda b, h, i: (b, i, h)),
        out_shape=jax.ShapeDtypeStruct((B, S, D), _BF16),
        scratch_shapes=[
            pltpu.VMEM((ATT_LANES, tq), _F32),
            pltpu.VMEM((n_heads, SUBLANES, tq), _F32),
        ],
        compiler_params=pltpu.CompilerParams(
            dimension_semantics=("parallel", "parallel", "parallel"),
            vmem_limit_bytes=VMEM_LIMIT_BYTES),
        name="sb_attention",
    )(q, kp, vt)


def _mix_out_kernel(o_ref, h_ref, w_ref, g_ref, b_ref, out_ref, *, tm, rt, alpha):
    def proj(i):
        return _dot(o_ref[0, i * rt:(i + 1) * rt, :], w_ref[...])

    m = proj(0)
    for i in range(tm // rt):
        nxt = proj(i + 1) if (i + 1) * rt < tm else None
        rows = slice(i * rt, (i + 1) * rt)
        out_ref[0, rows, :] = _layer_norm(alpha * h_ref[0, rows, :] + m, g_ref[...], b_ref[...])
        m = nxt


def _mix_out_ln(o, h, w_out, g, b, alpha):
    B, S, D = h.shape
    tm = min(MIX_ROWS, S)
    rt = min(MIX_SUBTILE, tm)
    return pl.pallas_call(
        functools.partial(_mix_out_kernel, tm=tm, rt=rt, alpha=alpha),
        grid=(B, S // tm),
        in_specs=[
            pl.BlockSpec((1, tm, D), lambda b, s: (b, s, 0)),
            pl.BlockSpec((1, tm, D), lambda b, s: (b, s, 0)),
            _resident((D, D), lambda b, s: (0, 0)),
            _resident((1, D), lambda b, s: (0, 0)),
            _resident((1, D), lambda b, s: (0, 0)),
        ],
        out_specs=pl.BlockSpec((1, tm, D), lambda b, s: (b, s, 0)),
        out_shape=jax.ShapeDtypeStruct((B, S, D), _F32),
        compiler_params=pltpu.CompilerParams(
            dimension_semantics=("parallel", "parallel"), vmem_limit_bytes=VMEM_LIMIT_BYTES),
        name="mix_out_ln",
    )(o, h, w_out, g, b)


def _gelu_tanh(x):
    c = math.sqrt(2.0 / math.pi)
    return 0.5 * x * (1.0 + jnp.tanh(c * (x + 0.044715 * (x * x * x))))


def _gmlp_kernel(x_ref, win_ref, lng_ref, lnb_ref, ws_ref, bs_ref, wout_ref, g_ref, b_ref,
                 out_ref, *, tm, rt, width, alpha):
    gw = width // GMLP_GROUPS
    tri = (lax.broadcasted_iota(jnp.int32, (GMLP_CHUNK, GMLP_CHUNK), 0)
           >= lax.broadcasted_iota(jnp.int32, (GMLP_CHUNK, GMLP_CHUNK), 1))
    w_causal = [jnp.where(tri, ws_ref[g], 0.0).astype(_BF16) for g in range(GMLP_GROUPS)]

    def proj_in(i):
        xb = x_ref[0, i * rt:(i + 1) * rt, :].astype(_BF16)
        return _dot(xb, win_ref[:, :width]), _dot(xb, win_ref[:, width:])

    zz = proj_in(0)
    for i in range(tm // rt):
        nxt = proj_in(i + 1) if (i + 1) * rt < tm else None
        u = _gelu_tanh(zz[0])
        v = _gelu_tanh(zz[1])
        vn = _layer_norm(v, lng_ref[...], lnb_ref[...]).astype(_BF16)
        chunks = []
        for ch in range(rt // GMLP_CHUNK):
            r0 = ch * GMLP_CHUNK
            cols = []
            for g in range(GMLP_GROUPS):
                cols.append(_dot(w_causal[g], vn[r0:r0 + GMLP_CHUNK, g * gw:(g + 1) * gw]))
            chunks.append(jnp.concatenate(cols, axis=1) + bs_ref[...])
        s = jnp.concatenate(chunks, axis=0)
        m = _dot((u * s).astype(_BF16), wout_ref[...])
        rows = slice(i * rt, (i + 1) * rt)
        out_ref[0, rows, :] = _layer_norm(alpha * x_ref[0, rows, :] + m, g_ref[...], b_ref[...])
        zz = nxt


def _gmlp_mixer(h, w_in, ln_g, ln_b, w_s, bs_full, w_out, g, b, alpha):
    B, S, D = h.shape
    width = w_out.shape[0]
    tm = min(GMLP_ROWS, S)
    rt = min(GMLP_SUBTILE, tm)
    const2 = lambda b, s: (0, 0)
    return pl.pallas_call(
        functools.partial(_gmlp_kernel, tm=tm, rt=rt, width=width, alpha=alpha),
        grid=(B, S // tm),
        in_specs=[
            pl.BlockSpec((1, tm, D), lambda b, s: (b, s, 0)),
            _resident((D, 2 * width), const2),
            _resident((1, width), const2),
            _resident((1, width), const2),
            _resident((GMLP_GROUPS, GMLP_CHUNK, GMLP_CHUNK), lambda b, s: (0, 0, 0)),
            _resident((GMLP_CHUNK, width), const2),
            _resident((width, D), const2),
            _resident((1, D), const2),
            _resident((1, D), const2),
        ],
        out_specs=pl.BlockSpec((1, tm, D), lambda b, s: (b, s, 0)),
        out_shape=jax.ShapeDtypeStruct((B, S, D), _F32),
        compiler_params=pltpu.CompilerParams(
            dimension_semantics=("parallel", "parallel"), vmem_limit_bytes=VMEM_LIMIT_BYTES),
        name="gmlp_mixer",
    )(h, w_in, ln_g, ln_b, w_s, bs_full, w_out, g, b)


PACKED_ROWS = 2 * SUBLANES


def _slab_tokens(v, zero):
    u = pltpu.bitcast(v, jnp.uint32)
    tokens = []
    for j in range(u.shape[0] // SUBLANES):
        for l in range(u.shape[1] // LANES):
            tokens.append(u[SUBLANES * j:SUBLANES * (j + 1), LANES * l:LANES * (l + 1)] & zero)
    return tokens


def _tie_to_tokens(x, tokens, k_tile):
    n_slabs = x.shape[0] // PACKED_ROWS
    n_lanes = x.shape[1] // LANES
    pieces = [[x[PACKED_ROWS * i:PACKED_ROWS * (i + 1), LANES * l:LANES * (l + 1)] for l in range(n_lanes)]
              for i in range(n_slabs)]
    n_pos = (x.shape[1] // k_tile) * n_slabs
    for j, tok in enumerate(tokens):
        p = j * n_pos // len(tokens)
        k, i = p // n_slabs, p % n_slabs
        l = k * (k_tile // LANES)
        pieces[i][l] = pieces[i][l] + pltpu.bitcast(tok, _BF16)
    return jnp.concatenate([jnp.concatenate(row, axis=1) for row in pieces], axis=0)


def _ffn_kernel(x_ref, wup_ref, cw_ref, cb_ref, wdown_ref, g_ref, b_ref, zero_ref, out_ref,
                acc_ref, tail_ref, xb_ref, a0_ref, a1_ref, a2_ref, a3_ref,
                gated0_ref, gated1_ref, gated2_ref, gated3_ref, *, tm, rs, n_chunks, alpha):
    a_refs = (a0_ref, a1_ref, a2_ref, a3_ref)
    gated_refs = (gated0_ref, gated1_ref, gated2_ref, gated3_ref)
    n_slabs = tm // rs
    n_items = n_chunks * n_slabs

    @pl.when(pl.program_id(1) == 0)
    def _():
        tail_ref[...] = jnp.zeros_like(tail_ref)

    def cast_slab(r):
        xb_ref[r * rs:(r + 1) * rs, :] = x_ref[0, r * rs:(r + 1) * rs, :].astype(_BF16)

    def item(t):
        return t // n_slabs, pl.multiple_of((t % n_slabs) * rs, rs)

    def up(t, slot, tokens=()):
        c, r0 = item(t)
        xs = xb_ref[pl.ds(r0, rs), :]
        if tokens:
            xs = _tie_to_tokens(xs, tokens, MXU_TILE)
        for half in range(2):
            a_refs[slot][half] = _dot(xs, wup_ref[half * n_chunks + c])

    def conv_gate(t, slot_a, slot_g):
        c, _ = item(t)

        def conv_half(half):
            idx = half * n_chunks + c
            a = a_refs[slot_a][half]
            prev = tail_ref[idx]
            tail_ref[idx] = a[rs - SUBLANES:, :]
            ext = jnp.concatenate([prev, a], axis=0)
            a1 = pltpu.roll(ext, 1, axis=0)[SUBLANES:, :]
            a2 = pltpu.roll(ext, 2, axis=0)[SUBLANES:, :]
            w = cw_ref[idx]
            return cb_ref[idx] + w[0:1, :] * a2 + w[1:2, :] * a1 + w[2:3, :] * a

        gate = conv_half(0)
        val = conv_half(1)
        gated = (gate * jax.nn.sigmoid(gate) * val).astype(_BF16)
        gated_refs[slot_g][...] = gated
        return _slab_tokens(gated, zero_ref[...])

    def down(t, slot):
        c, r0 = item(t)
        acc_ref[pl.ds(r0, rs), :] += _dot(gated_refs[slot][...], wdown_ref[c])

    def stage(t, k, do_down=True, do_up=True):
        tokens = conv_gate(t, k % 4, k % 4)
        if do_up:
            up(t + 2, (k + 2) % 4, tokens)
        if do_down:
            down(t - 1, (k - 1) % 4)

    cast_slab(0)
    up(0, 0)
    if n_slabs > 1:
        cast_slab(1)
    up(1, 1)
    for r in range(2, n_slabs):
        cast_slab(r)
    acc_ref[...] = jnp.zeros_like(acc_ref)
    stage(0, 0, do_down=False)
    stage(1, 1)

    def group(q, carry):
        t = FFN_UNROLL * q + 2
        for k in range(FFN_UNROLL):
            stage(t + k, (2 + k) % 4)
        return carry

    lax.fori_loop(0, (n_items - 4) // FFN_UNROLL, group, 0)
    stage(n_items - 2, (n_items - 2) % 4, do_up=False)
    stage(n_items - 1, (n_items - 1) % 4, do_up=False)
    down(n_items - 1, (n_items - 1) % 4)
    out_ref[0] = _layer_norm(alpha * x_ref[0] + acc_ref[...], g_ref[...], b_ref[...])


def _conv_ffn(h, wup_c, cw_c, cb_c, wdown_c, g, b, alpha):
    B, S, D = h.shape
    n2, _, fc = wup_c.shape
    n_chunks = n2 // 2
    tm = min(FFN_ROWS, S)
    rs = min(FFN_SLAB, tm)
    n_items = n_chunks * (tm // rs)
    assert n_items >= 4 + FFN_UNROLL and (n_items - 4) % FFN_UNROLL == 0
    const2 = lambda b, s: (0, 0)
    const3 = lambda b, s: (0, 0, 0)
    return pl.pallas_call(
        functools.partial(_ffn_kernel, tm=tm, rs=rs, n_chunks=n_chunks, alpha=alpha),
        grid=(B, S // tm),
        in_specs=[
            pl.BlockSpec((1, tm, D), lambda b, s: (b, s, 0)),
            _resident((n2, D, fc), const3),
            _resident((n2, CONV_WIDTH, fc), const3),
            _resident((n2, 1, fc), const3),
            _resident((n_chunks, fc, D), const3),
            _resident((1, D), const2),
            _resident((1, D), const2),
            _resident((SUBLANES, LANES), const2),
        ],
        out_specs=pl.BlockSpec((1, tm, D), lambda b, s: (b, s, 0)),
        out_shape=jax.ShapeDtypeStruct((B, S, D), _F32),
        scratch_shapes=[
            pltpu.VMEM((tm, D), _F32),
            pltpu.VMEM((n2, SUBLANES, fc), _F32),
            pltpu.VMEM((tm, D), _BF16),
            pltpu.VMEM((2, rs, fc), _F32),
            pltpu.VMEM((2, rs, fc), _F32),
            pltpu.VMEM((2, rs, fc), _F32),
            pltpu.VMEM((2, rs, fc), _F32),
            pltpu.VMEM((rs, fc), _BF16),
            pltpu.VMEM((rs, fc), _BF16),
            pltpu.VMEM((rs, fc), _BF16),
            pltpu.VMEM((rs, fc), _BF16),
        ],
        compiler_params=pltpu.CompilerParams(
            dimension_semantics=("parallel", "arbitrary"), vmem_limit_bytes=VMEM_LIMIT_BYTES),
        name="conv_ffn",
    )(h, wup_c, cw_c, cb_c, wdown_c, g, b, jnp.zeros((SUBLANES, LANES), jnp.uint32))


MIX_ROWS = 1024
MIX_SUBTILE = 256
GMLP_ROWS = 512
GMLP_SUBTILE = 256
FFN_ROWS = 1024
FFN_SLAB = 256
FFN_UNROLL = 8
FFN_CHUNK = 256


def _chunk_cols(w, fc):
    r, c = w.shape
    return jnp.transpose(w.reshape(r, c // fc, fc), (1, 0, 2))


def kernel(x, attn_w_in, attn_w_out, gmlp_w_in, gmlp_ln_g, gmlp_ln_b, gmlp_w_s, gmlp_b_s, gmlp_w_out,
           ffn_w_up, ffn_conv_w, ffn_conv_b, ffn_w_down, ln_mix_g, ln_mix_b, ln_ffn_g, ln_ffn_b):
    B, S, D = x.shape
    depth = ffn_w_up.shape[0]
    alpha = (2 * depth) ** 0.25
    d_ff = ffn_w_down.shape[1]
    width = gmlp_w_out.shape[1]
    row = lambda v: v.reshape(1, -1)

    h = x
    for i in range(depth):
        j = i // 2
        if i % 2 == 0:
            w_in = attn_w_in[j]
            wq = w_in[:, :D].astype(_BF16)
            wk = w_in[:, D:2 * D].astype(_BF16)
            wvt = w_in[:, 2 * D:].T.astype(_BF16)
            q, kp, vt = _qkv_proj(h, wq, wk, wvt)
            o = _sb_attention(q, kp, vt)
            h = _mix_out_ln(o, h, attn_w_out[j].astype(_BF16), row(ln_mix_g[i]), row(ln_mix_b[i]), alpha)
        else:
            bs_full = jnp.repeat(gmlp_b_s[j].T, width // GMLP_GROUPS, axis=1)
            h = _gmlp_mixer(h, gmlp_w_in[j].astype(_BF16), row(gmlp_ln_g[j]), row(gmlp_ln_b[j]),
                            gmlp_w_s[j], bs_full, gmlp_w_out[j].astype(_BF16),
                            row(ln_mix_g[i]), row(ln_mix_b[i]), alpha)
        wup_c = _chunk_cols(ffn_w_up[i].astype(_BF16), FFN_CHUNK)
        cw_c = _chunk_cols(ffn_conv_w[i], FFN_CHUNK)
        cb_c = _chunk_cols(ffn_conv_b[i].reshape(1, -1), FFN_CHUNK)
        wdown_c = ffn_w_down[i].astype(_BF16).reshape(d_ff // FFN_CHUNK, FFN_CHUNK, D)
        h = _conv_ffn(h, wup_c, cw_c, cb_c, wdown_c, row(ln_ffn_g[i]), row(ln_ffn_b[i]), alpha)
    return h
```

```python
import functools
import math

import jax
import jax.numpy as jnp
from jax import lax
from jax.experimental import pallas as pl
from jax.experimental.pallas import tpu as pltpu

LN_EPS = 1e-5
CONV_WIDTH = 3
SB_HEAD_DIM = 64
GMLP_GROUPS = 8
GMLP_CHUNK = 128

SUBLANES = 8
LANES = 128
MXU_TILE = 256
KEY_BLOCK = 128
KEY_GROUP = KEY_BLOCK // SUBLANES
VMEM_LIMIT_BYTES = 56 * 1024 * 1024

ATT_TQ = 256
ATT_KB = 256
ATT_LANES = 1024
ATT_CHAIN = 4
LOG2E = 1.0 / math.log(2.0)
INV_LN2 = 1.0 / math.log(2.0)
LOG2_DEAD = 150.0

_BF16 = jnp.bfloat16
_F32 = jnp.float32


def _resident(block_shape, index_map):
    return pl.BlockSpec(block_shape, index_map, pipeline_mode=pl.Buffered(1))


def _layer_norm(y, g, b):
    mu = jnp.mean(y, axis=-1, keepdims=True)
    d = y - mu
    var = jnp.mean(d * d, axis=-1, keepdims=True)
    return d * lax.rsqrt(var + LN_EPS) * g + b


def _dot(a, b):
    return jnp.dot(a, b, preferred_element_type=_F32)


def _dot_nt(a, b):
    return lax.dot_general(a, b, (((1,), (1,)), ((), ())), preferred_element_type=_F32)


def _qkv_kernel(h_ref, wq_ref, wk_ref, wvt_ref, q_ref, kp_ref, vt_ref, *, tm, scale):
    hb = h_ref[0].astype(_BF16)
    row = lax.broadcasted_iota(jnp.int32, (KEY_BLOCK, KEY_BLOCK), 0)
    col = lax.broadcasted_iota(jnp.int32, (KEY_BLOCK, KEY_BLOCK), 1)
    perm = (col == (row % SUBLANES) * KEY_GROUP + row // SUBLANES).astype(_BF16)
    hp = jnp.concatenate(
        [_dot(perm, hb[blk * KEY_BLOCK:(blk + 1) * KEY_BLOCK, :]) for blk in range(tm // KEY_BLOCK)],
        axis=0).astype(_BF16)
    q_ref[0] = (_dot(hb, wq_ref[...]) * scale).astype(_BF16)
    kp_ref[0] = _dot(hp, wk_ref[...]).astype(_BF16)
    vt_ref[0] = _dot_nt(wvt_ref[...], hp).astype(_BF16)


def _qkv_proj(h, wq, wk, wvt):
    B, S, D = h.shape
    tm = min(512, S)
    scale = SB_HEAD_DIM ** -0.5 * LOG2E
    return pl.pallas_call(
        functools.partial(_qkv_kernel, tm=tm, scale=scale),
        grid=(B, S // tm),
        in_specs=[
            pl.BlockSpec((1, tm, D), lambda b, s: (b, s, 0)),
            _resident((D, D), lambda b, s: (0, 0)),
            _resident((D, D), lambda b, s: (0, 0)),
            _resident((D, D), lambda b, s: (0, 0)),
        ],
        out_specs=[
            pl.BlockSpec((1, tm, D), lambda b, s: (b, s, 0)),
            pl.BlockSpec((1, tm, D), lambda b, s: (b, s, 0)),
            pl.BlockSpec((1, D, tm), lambda b, s: (b, 0, s)),
        ],
        out_shape=[
            jax.ShapeDtypeStruct((B, S, D), _BF16),
            jax.ShapeDtypeStruct((B, S, D), _BF16),
            jax.ShapeDtypeStruct((B, D, S), _BF16),
        ],
        compiler_params=pltpu.CompilerParams(
            dimension_semantics=("parallel", "parallel"), vmem_limit_bytes=VMEM_LIMIT_BYTES),
        name="qkv_proj",
    )(h, wq, wk, wvt)


def _sublane_suffix_scan(g):
    row = lax.broadcasted_iota(jnp.int32, g.shape, 0)
    x = g
    for sh in (1, 2, 4):
        shifted = pltpu.roll(x, SUBLANES - sh, axis=0)
        x = x + jnp.where(row + sh < SUBLANES, shifted, 0.0)
    return x


def _sb_block(z, carry, causal):
    m = jnp.maximum(z, 0.0) + jnp.log(1.0 + jnp.exp2(-jnp.abs(z))) * INV_LN2
    if causal is not None:
        m = jnp.where(causal, m, 0.0)
    tiles = [m[SUBLANES * v:SUBLANES * (v + 1), :] for v in range(KEY_GROUP)]
    run = [None] * KEY_GROUP
    run[KEY_GROUP - 1] = tiles[KEY_GROUP - 1]
    for v in range(KEY_GROUP - 2, -1, -1):
        run[v] = run[v + 1] + tiles[v]
    group_tot = run[0]
    incl = _sublane_suffix_scan(group_tot)
    base = (incl - group_tot) + carry
    total = jnp.concatenate([run[v] + base for v in range(KEY_GROUP)], axis=0)
    a = jnp.exp2(z - total)
    if causal is not None:
        a = jnp.where(causal, a, 0.0)
    new_carry = carry + jnp.broadcast_to(incl[0:1, :], carry.shape)
    return a.astype(_BF16), new_carry


def _attn_kernel(q_ref, kp_ref, vt_ref, o_ref, acc_ref, carry_ref, *, tq, n_heads):
    qi = pl.program_id(2)
    lane = lax.broadcasted_iota(jnp.int32, (tq, LANES), 1)
    qm = []
    for h in range(n_heads):
        qpair = q_ref[0, :, LANES * (h // 2):LANES * (h // 2 + 1)]
        qm.append(jnp.where((lane // SB_HEAD_DIM) == h % 2, qpair, jnp.zeros_like(qpair)))

    acc_ref[...] = jnp.zeros_like(acc_ref)
    carry_ref[...] = jnp.zeros_like(carry_ref)

    def causal_mask(sub, q0):
        row = lax.broadcasted_iota(jnp.int32, (KEY_BLOCK, tq - q0), 0)
        key_off = (row % SUBLANES) * KEY_GROUP + row // SUBLANES + sub * KEY_BLOCK
        return key_off < lax.broadcasted_iota(jnp.int32, (KEY_BLOCK, tq - q0), 1) + q0

    def key_step(j, diagonal):
        n_sub = ATT_KB // KEY_BLOCK
        first_q = [sub * KEY_BLOCK if diagonal else 0 for sub in range(n_sub)]

        def scores(h):
            out = [None] * n_sub
            for sub in range(n_sub - 1, -1, -1):
                start = pl.multiple_of(j * ATT_KB + sub * KEY_BLOCK, KEY_BLOCK)
                kblk = kp_ref[0, pl.ds(start, KEY_BLOCK), LANES * (h // 2):LANES * (h // 2 + 1)]
                out[sub] = _dot_nt(kblk, qm[h][first_q[sub]:, :])
            return out

        z = {h: scores(h) for h in range(min(ATT_CHAIN, n_heads))}
        for h in range(n_heads):
            a_parts = [None] * n_sub
            for sub in range(n_sub - 1, -1, -1):
                q0 = first_q[sub]
                mask = causal_mask(sub, q0) if diagonal else None
                carry = carry_ref[h]
                a, new_carry = _sb_block(z[h][sub], carry[:, q0:], mask)
                if q0:
                    a = jnp.concatenate([jnp.zeros((KEY_BLOCK, q0), a.dtype), a], axis=1)
                    new_carry = jnp.concatenate([carry[:, :q0], new_carry], axis=1)
                a_parts[sub], carry_ref[h] = a, new_carry
            if h + ATT_CHAIN < n_heads:
                z[h + ATT_CHAIN] = scores(h + ATT_CHAIN)
            vt = vt_ref[0, SB_HEAD_DIM * h:SB_HEAD_DIM * (h + 1),
                        pl.ds(pl.multiple_of(j * ATT_KB, ATT_KB), ATT_KB)]
            acc_ref[SB_HEAD_DIM * h:SB_HEAD_DIM * (h + 1), :] += _dot(vt, jnp.concatenate(a_parts, axis=0))

    def any_alive():
        return jnp.min(carry_ref[...]) < LOG2_DEAD

    key_step(qi, True)

    def body(state):
        j, _ = state
        key_step(j, False)
        return j - 1, any_alive()

    lax.while_loop(lambda st: jnp.logical_and(st[0] >= 0, st[1]), body, (qi - 1, any_alive()))
    o_ref[0] = acc_ref[...].T.astype(o_ref.dtype)


def _sb_attention(q, kp, vt):
    B, S, D = q.shape
    tq = min(ATT_TQ, S)
    assert tq == ATT_KB and S % tq == 0 and D % ATT_LANES == 0
    n_heads = ATT_LANES // SB_HEAD_DIM
    return pl.pallas_call(
        functools.partial(_attn_kernel, tq=tq, n_heads=n_heads),
        grid=(B, D // ATT_LANES, S // tq),
        in_specs=[
            pl.BlockSpec((1, tq, ATT_LANES), lambda b, h, i: (b, i, h)),
            pl.BlockSpec((1, S, ATT_LANES), lambda b, h, i: (b, 0, h)),
            pl.BlockSpec((1, ATT_LANES, S), lambda b, h, i: (b, h, 0)),
        ],
        out_specs=pl.BlockSpec((1, tq, ATT_LANES), lambda b, h, i: (b, i, h)),
        out_shape=jax.ShapeDtypeStruct((B, S, D), _BF16),
        scratch_shapes=[
            pltpu.VMEM((ATT_LANES, tq), _F32),
            pltpu.VMEM((n_heads, SUBLANES, tq), _F32),
        ],
        compiler_params=pltpu.CompilerParams(
            dimension_semantics=("parallel", "parallel", "parallel"),
            vmem_limit_bytes=VMEM_LIMIT_BYTES),
        name="sb_attention",
    )(q, kp, vt)


def _mix_out_kernel(o_ref, h_ref, w_ref, g_ref, b_ref, out_ref, *, tm, rt, alpha):
    def proj(i):
        return _dot(o_ref[0, i * rt:(i + 1) * rt, :], w_ref[...])

    m = proj(0)
    for i in range(tm // rt):
        nxt = proj(i + 1) if (i + 1) * rt < tm else None
        rows = slice(i * rt, (i + 1) * rt)
        out_ref[0, rows, :] = _layer_norm(alpha * h_ref[0, rows, :] + m, g_ref[...], b_ref[...])
        m = nxt


def _mix_out_ln(o, h, w_out, g, b, alpha):
    B, S, D = h.shape
    tm = min(MIX_ROWS, S)
    rt = min(MIX_SUBTILE, tm)
    return pl.pallas_call(
        functools.partial(_mix_out_kernel, tm=tm, rt=rt, alpha=alpha),
        grid=(B, S // tm),
        in_specs=[
            pl.BlockSpec((1, tm, D), lambda b, s: (b, s, 0)),
            pl.BlockSpec((1, tm, D), lambda b, s: (b, s, 0)),
            _resident((D, D), lambda b, s: (0, 0)),
            _resident((1, D), lambda b, s: (0, 0)),
            _resident((1, D), lambda b, s: (0, 0)),
        ],
        out_specs=pl.BlockSpec((1, tm, D), lambda b, s: (b, s, 0)),
        out_shape=jax.ShapeDtypeStruct((B, S, D), _F32),
        compiler_params=pltpu.CompilerParams(
            dimension_semantics=("parallel", "parallel"), vmem_limit_bytes=VMEM_LIMIT_BYTES),
        name="mix_out_ln",
    )(o, h, w_out, g, b)


def _gelu_tanh(x):
    c = math.sqrt(2.0 / math.pi)
    return 0.5 * x * (1.0 + jnp.tanh(c * (x + 0.044715 * (x * x * x))))


def _gmlp_kernel(x_ref, win_ref, lng_ref, lnb_ref, ws_ref, bs_ref, wout_ref, g_ref, b_ref,
                 out_ref, *, tm, rt, width, alpha):
    gw = width // GMLP_GROUPS
    tri = (lax.broadcasted_iota(jnp.int32, (GMLP_CHUNK, GMLP_CHUNK), 0)
           >= lax.broadcasted_iota(jnp.int32, (GMLP_CHUNK, GMLP_CHUNK), 1))
    w_causal = [jnp.where(tri, ws_ref[g], 0.0).astype(_BF16) for g in range(GMLP_GROUPS)]

    def proj_in(i):
        xb = x_ref[0, i * rt:(i + 1) * rt, :].astype(_BF16)
        return _dot(xb, win_ref[:, :width]), _dot(xb, win_ref[:, width:])

    zz = proj_in(0)
    for i in range(tm // rt):
        nxt = proj_in(i + 1) if (i + 1) * rt < tm else None
        u = _gelu_tanh(zz[0])
        v = _gelu_tanh(zz[1])
        vn = _layer_norm(v, lng_ref[...], lnb_ref[...]).astype(_BF16)
        chunks = []
        for ch in range(rt // GMLP_CHUNK):
            r0 = ch * GMLP_CHUNK
            cols = []
            for g in range(GMLP_GROUPS):
                cols.append(_dot(w_causal[g], vn[r0:r0 + GMLP_CHUNK, g * gw:(g + 1) * gw]))
            chunks.append(jnp.concatenate(cols, axis=1) + bs_ref[...])
        s = jnp.concatenate(chunks, axis=0)
        m = _dot((u * s).astype(_BF16), wout_ref[...])
        rows = slice(i * rt, (i + 1) * rt)
        out_ref[0, rows, :] = _layer_norm(alpha * x_ref[0, rows, :] + m, g_ref[...], b_ref[...])
        zz = nxt


def _gmlp_mixer(h, w_in, ln_g, ln_b, w_s, bs_full, w_out, g, b, alpha):
    B, S, D = h.shape
    width = w_out.shape[0]
    tm = min(GMLP_ROWS, S)
    rt = min(GMLP_SUBTILE, tm)
    const2 = lambda b, s: (0, 0)
    return pl.pallas_call(
        functools.partial(_gmlp_kernel, tm=tm, rt=rt, width=width, alpha=alpha),
        grid=(B, S // tm),
        in_specs=[
            pl.BlockSpec((1, tm, D), lambda b, s: (b, s, 0)),
            _resident((D, 2 * width), const2),
            _resident((1, width), const2),
            _resident((1, width), const2),
            _resident((GMLP_GROUPS, GMLP_CHUNK, GMLP_CHUNK), lambda b, s: (0, 0, 0)),
            _resident((GMLP_CHUNK, width), const2),
            _resident((width, D), const2),
            _resident((1, D), const2),
            _resident((1, D), const2),
        ],
        out_specs=pl.BlockSpec((1, tm, D), lambda b, s: (b, s, 0)),
        out_shape=jax.ShapeDtypeStruct((B, S, D), _F32),
        compiler_params=pltpu.CompilerParams(
            dimension_semantics=("parallel", "parallel"), vmem_limit_bytes=VMEM_LIMIT_BYTES),
        name="gmlp_mixer",
    )(h, w_in, ln_g, ln_b, w_s, bs_full, w_out, g, b)


PACKED_ROWS = 2 * SUBLANES


def _slab_tokens(v, zero):
    u = pltpu.bitcast(v, jnp.uint32)
    tokens = []
    for j in range(u.shape[0] // SUBLANES):
        for l in range(u.shape[1] // LANES):
            tokens.append(u[SUBLANES * j:SUBLANES * (j + 1), LANES * l:LANES * (l + 1)] & zero)
    return tokens


def _tie_to_tokens(x, tokens, k_tile):
    n_slabs = x.shape[0] // PACKED_ROWS
    n_lanes = x.shape[1] // LANES
    pieces = [[x[PACKED_ROWS * i:PACKED_ROWS * (i + 1), LANES * l:LANES * (l + 1)] for l in range(n_lanes)]
              for i in range(n_slabs)]
    n_pos = (x.shape[1] // k_tile) * n_slabs
    for j, tok in enumerate(tokens):
        p = j * n_pos // len(tokens)
        k, i = p // n_slabs, p % n_slabs
        l = k * (k_tile // LANES)
        pieces[i][l] = pieces[i][l] + pltpu.bitcast(tok, _BF16)
    return jnp.concatenate([jnp.concatenate(row, axis=1) for row in pieces], axis=0)


def _ffn_kernel(x_ref, wup_ref, cw_ref, cb_ref, wdown_ref, g_ref, b_ref, zero_ref, out_ref,
                acc_ref, tail_ref, xb_ref, a0_ref, a1_ref, a2_ref, a3_ref,
                gated0_ref, gated1_ref, gated2_ref, gated3_ref, *, tm, rs, n_chunks, alpha):
    a_refs = (a0_ref, a1_ref, a2_ref, a3_ref)
    gated_refs = (gated0_ref, gated1_ref, gated2_ref, gated3_ref)
    n_slabs = tm // rs
    n_items = n_chunks * n_slabs

    @pl.when(pl.program_id(1) == 0)
    def _():
        tail_ref[...] = jnp.zeros_like(tail_ref)

    def cast_slab(r):
        xb_ref[r * rs:(r + 1) * rs, :] = x_ref[0, r * rs:(r + 1) * rs, :].astype(_BF16)

    def item(t):
        return t // n_slabs, pl.multiple_of((t % n_slabs) * rs, rs)

    def up(t, slot, tokens=()):
        c, r0 = item(t)
        xs = xb_ref[pl.ds(r0, rs), :]
        if tokens:
            xs = _tie_to_tokens(xs, tokens, MXU_TILE)
        for half in range(2):
            a_refs[slot][half] = _dot(xs, wup_ref[half * n_chunks + c])

    def conv_gate(t, slot_a, slot_g):
        c, _ = item(t)

        def conv_half(half):
            idx = half * n_chunks + c
            a = a_refs[slot_a][half]
            prev = tail_ref[idx]
            tail_ref[idx] = a[rs - SUBLANES:, :]
            ext = jnp.concatenate([prev, a], axis=0)
            a1 = pltpu.roll(ext, 1, axis=0)[SUBLANES:, :]
            a2 = pltpu.roll(ext, 2, axis=0)[SUBLANES:, :]
            w = cw_ref[idx]
            return cb_ref[idx] + w[0:1, :] * a2 + w[1:2, :] * a1 + w[2:3, :] * a

        gate = conv_half(0)
        val = conv_half(1)
        gated = (gate * jax.nn.sigmoid(gate) * val).astype(_BF16)
        gated_refs[slot_g][...] = gated
        return _slab_tokens(gated, zero_ref[...])

    def down(t, slot):
        c, r0 = item(t)
        acc_ref[pl.ds(r0, rs), :] += _dot(gated_refs[slot][...], wdown_ref[c])

    def stage(t, k, do_down=True, do_up=True):
        tokens = conv_gate(t, k % 4, k % 4)
        if do_up:
            up(t + 2, (k + 2) % 4, tokens)
        if do_down:
            down(t - 1, (k - 1) % 4)

    cast_slab(0)
    up(0, 0)
    if n_slabs > 1:
        cast_slab(1)
    up(1, 1)
    for r in range(2, n_slabs):
        cast_slab(r)
    acc_ref[...] = jnp.zeros_like(acc_ref)
    stage(0, 0, do_down=False)
    stage(1, 1)

    def group(q, carry):
        t = FFN_UNROLL * q + 2
        for k in range(FFN_UNROLL):
            stage(t + k, (2 + k) % 4)
        return carry

    lax.fori_loop(0, (n_items - 4) // FFN_UNROLL, group, 0)
    stage(n_items - 2, (n_items - 2) % 4, do_up=False)
    stage(n_items - 1, (n_items - 1) % 4, do_up=False)
    down(n_items - 1, (n_items - 1) % 4)
    out_ref[0] = _layer_norm(alpha * x_ref[0] + acc_ref[...], g_ref[...], b_ref[...])


def _conv_ffn(h, wup_c, cw_c, cb_c, wdown_c, g, b, alpha):
    B, S, D = h.shape
    n2, _, fc = wup_c.shape
    n_chunks = n2 // 2
    tm = min(FFN_ROWS, S)
    rs = min(FFN_SLAB, tm)
    n_items = n_chunks * (tm // rs)
    assert n_items >= 4 + FFN_UNROLL and (n_items - 4) % FFN_UNROLL == 0
    const2 = lambda b, s: (0, 0)
    const3 = lambda b, s: (0, 0, 0)
    return pl.pallas_call(
        functools.partial(_ffn_kernel, tm=tm, rs=rs, n_chunks=n_chunks, alpha=alpha),
        grid=(B, S // tm),
        in_specs=[
            pl.BlockSpec((1, tm, D), lambda b, s: (b, s, 0)),
            _resident((n2, D, fc), const3),
            _resident((n2, CONV_WIDTH, fc), const3),
            _resident((n2, 1, fc), const3),
            _resident((n_chunks, fc, D), const3),
            _resident((1, D), const2),
            _resident((1, D), const2),
            _resident((SUBLANES, LANES), const2),
        ],
        out_specs=pl.BlockSpec((1, tm, D), lambda b, s: (b, s, 0)),
        out_shape=jax.ShapeDtypeStruct((B, S, D), _F32),
        scratch_shapes=[
            pltpu.VMEM((tm, D), _F32),
            pltpu.VMEM((n2, SUBLANES, fc), _F32),
            pltpu.VMEM((tm, D), _BF16),
            pltpu.VMEM((2, rs, fc), _F32),
            pltpu.VMEM((2, rs, fc), _F32),
            pltpu.VMEM((2, rs, fc), _F32),
            pltpu.VMEM((2, rs, fc), _F32),
            pltpu.VMEM((rs, fc), _BF16),
            pltpu.VMEM((rs, fc), _BF16),
            pltpu.VMEM((rs, fc), _BF16),
            pltpu.VMEM((rs, fc), _BF16),
        ],
        compiler_params=pltpu.CompilerParams(
            dimension_semantics=("parallel", "arbitrary"), vmem_limit_bytes=VMEM_LIMIT_BYTES),
        name="conv_ffn",
    )(h, wup_c, cw_c, cb_c, wdown_c, g, b, jnp.zeros((SUBLANES, LANES), jnp.uint32))


MIX_ROWS = 1024
MIX_SUBTILE = 256
GMLP_ROWS = 512
GMLP_SUBTILE = 256
FFN_ROWS = 1024
FFN_SLAB = 256
FFN_UNROLL = 20
FFN_CHUNK = 256


def _chunk_cols(w, fc):
    r, c = w.shape
    return jnp.transpose(w.reshape(r, c // fc, fc), (1, 0, 2))


def kernel(x, attn_w_in, attn_w_out, gmlp_w_in, gmlp_ln_g, gmlp_ln_b, gmlp_w_s, gmlp_b_s, gmlp_w_out,
           ffn_w_up, ffn_conv_w, ffn_conv_b, ffn_w_down, ln_mix_g, ln_mix_b, ln_ffn_g, ln_ffn_b):
    B, S, D = x.shape
    depth = ffn_w_up.shape[0]
    alpha = (2 * depth) ** 0.25
    d_ff = ffn_w_down.shape[1]
    width = gmlp_w_out.shape[1]
    row = lambda v: v.reshape(1, -1)

    h = x
    for i in range(depth):
        j = i // 2
        if i % 2 == 0:
            w_in = attn_w_in[j]
            wq = w_in[:, :D].astype(_BF16)
            wk = w_in[:, D:2 * D].astype(_BF16)
            wvt = w_in[:, 2 * D:].T.astype(_BF16)
            q, kp, vt = _qkv_proj(h, wq, wk, wvt)
            o = _sb_attention(q, kp, vt)
            h = _mix_out_ln(o, h, attn_w_out[j].astype(_BF16), row(ln_mix_g[i]), row(ln_mix_b[i]), alpha)
        else:
            bs_full = jnp.repeat(gmlp_b_s[j].T, width // GMLP_GROUPS, axis=1)
            h = _gmlp_mixer(h, gmlp_w_in[j].astype(_BF16), row(gmlp_ln_g[j]), row(gmlp_ln_b[j]),
                            gmlp_w_s[j], bs_full, gmlp_w_out[j].astype(_BF16),
                            row(ln_mix_g[i]), row(ln_mix_b[i]), alpha)
        wup_c = _chunk_cols(ffn_w_up[i].astype(_BF16), FFN_CHUNK)
        cw_c = _chunk_cols(ffn_conv_w[i], FFN_CHUNK)
        cb_c = _chunk_cols(ffn_conv_b[i].reshape(1, -1), FFN_CHUNK)
        wdown_c = ffn_w_down[i].astype(_BF16).reshape(d_ff // FFN_CHUNK, FFN_CHUNK, D)
        h = _conv_ffn(h, wup_c, cw_c, cb_c, wdown_c, row(ln_ffn_g[i]), row(ln_ffn_b[i]), alpha)
    return h
```

```python
import functools
import math

import jax
import jax.numpy as jnp
from jax import lax
from jax.experimental import pallas as pl
from jax.experimental.pallas import tpu as pltpu

LN_EPS = 1e-5
CONV_WIDTH = 3
SB_HEAD_DIM = 64
GMLP_GROUPS = 8
GMLP_CHUNK = 128

SUBLANES = 8
LANES = 128
MXU_TILE = 256
KEY_BLOCK = 128
KEY_GROUP = KEY_BLOCK // SUBLANES
VMEM_LIMIT_BYTES = 56 * 1024 * 1024

ATT_TQ = 256
ATT_KB = 256
ATT_LANES = 1024
ATT_CHAIN = 5
LOG2E = 1.0 / math.log(2.0)
INV_LN2 = 1.0 / math.log(2.0)
LOG2_DEAD = 150.0

_BF16 = jnp.bfloat16
_F32 = jnp.float32


def _resident(block_shape, index_map):
    return pl.BlockSpec(block_shape, index_map, pipeline_mode=pl.Buffered(1))


def _layer_norm(y, g, b):
    mu = jnp.mean(y, axis=-1, keepdims=True)
    d = y - mu
    var = jnp.mean(d * d, axis=-1, keepdims=True)
    return d * lax.rsqrt(var + LN_EPS) * g + b


def _dot(a, b):
    return jnp.dot(a, b, preferred_element_type=_F32)


def _dot_nt(a, b):
    return lax.dot_general(a, b, (((1,), (1,)), ((), ())), preferred_element_type=_F32)


def _qkv_kernel(h_ref, wq_ref, wk_ref, wvt_ref, q_ref, kp_ref, vt_ref, *, tm, scale):
    hb = h_ref[0].astype(_BF16)
    row = lax.broadcasted_iota(jnp.int32, (KEY_BLOCK, KEY_BLOCK), 0)
    col = lax.broadcasted_iota(jnp.int32, (KEY_BLOCK, KEY_BLOCK), 1)
    perm = (col == (row % SUBLANES) * KEY_GROUP + row // SUBLANES).astype(_BF16)
    hp = jnp.concatenate(
        [_dot(perm, hb[blk * KEY_BLOCK:(blk + 1) * KEY_BLOCK, :]) for blk in range(tm // KEY_BLOCK)],
        axis=0).astype(_BF16)
    q_ref[0] = (_dot(hb, wq_ref[...]) * scale).astype(_BF16)
    kp_ref[0] = _dot(hp, wk_ref[...]).astype(_BF16)
    vt_ref[0] = _dot_nt(wvt_ref[...], hp).astype(_BF16)


def _qkv_proj(h, wq, wk, wvt):
    B, S, D = h.shape
    tm = min(512, S)
    scale = SB_HEAD_DIM ** -0.5 * LOG2E
    return pl.pallas_call(
        functools.partial(_qkv_kernel, tm=tm, scale=scale),
        grid=(B, S // tm),
        in_specs=[
            pl.BlockSpec((1, tm, D), lambda b, s: (b, s, 0)),
            _resident((D, D), lambda b, s: (0, 0)),
            _resident((D, D), lambda b, s: (0, 0)),
            _resident((D, D), lambda b, s: (0, 0)),
        ],
        out_specs=[
            pl.BlockSpec((1, tm, D), lambda b, s: (b, s, 0)),
            pl.BlockSpec((1, tm, D), lambda b, s: (b, s, 0)),
            pl.BlockSpec((1, D, tm), lambda b, s: (b, 0, s)),
        ],
        out_shape=[
            jax.ShapeDtypeStruct((B, S, D), _BF16),
            jax.ShapeDtypeStruct((B, S, D), _BF16),
            jax.ShapeDtypeStruct((B, D, S), _BF16),
        ],
        compiler_params=pltpu.CompilerParams(
            dimension_semantics=("parallel", "parallel"), vmem_limit_bytes=VMEM_LIMIT_BYTES),
        name="qkv_proj",
    )(h, wq, wk, wvt)


def _sublane_suffix_scan(g):
    row = lax.broadcasted_iota(jnp.int32, g.shape, 0)
    x = g
    for sh in (1, 2, 4):
        shifted = pltpu.roll(x, SUBLANES - sh, axis=0)
        x = x + jnp.where(row + sh < SUBLANES, shifted, 0.0)
    return x


def _sb_block(z, carry, causal):
    m = jnp.maximum(z, 0.0) + jnp.log(1.0 + jnp.exp2(-jnp.abs(z))) * INV_LN2
    if causal is not None:
        m = jnp.where(causal, m, 0.0)
    tiles = [m[SUBLANES * v:SUBLANES * (v + 1), :] for v in range(KEY_GROUP)]
    run = [None] * KEY_GROUP
    run[KEY_GROUP - 1] = tiles[KEY_GROUP - 1]
    for v in range(KEY_GROUP - 2, -1, -1):
        run[v] = run[v + 1] + tiles[v]
    group_tot = run[0]
    incl = _sublane_suffix_scan(group_tot)
    base = (incl - group_tot) + carry
    total = jnp.concatenate([run[v] + base for v in range(KEY_GROUP)], axis=0)
    a = jnp.exp2(z - total)
    if causal is not None:
        a = jnp.where(causal, a, 0.0)
    new_carry = carry + jnp.broadcast_to(incl[0:1, :], carry.shape)
    return a.astype(_BF16), new_carry


def _attn_kernel(q_ref, kp_ref, vt_ref, o_ref, acc_ref, carry_ref, *, tq, n_heads):
    qi = pl.program_id(2)
    lane = lax.broadcasted_iota(jnp.int32, (tq, LANES), 1)
    qm = []
    for h in range(n_heads):
        qpair = q_ref[0, :, LANES * (h // 2):LANES * (h // 2 + 1)]
        qm.append(jnp.where((lane // SB_HEAD_DIM) == h % 2, qpair, jnp.zeros_like(qpair)))

    acc_ref[...] = jnp.zeros_like(acc_ref)
    carry_ref[...] = jnp.zeros_like(carry_ref)

    def causal_mask(sub, q0):
        row = lax.broadcasted_iota(jnp.int32, (KEY_BLOCK, tq - q0), 0)
        key_off = (row % SUBLANES) * KEY_GROUP + row // SUBLANES + sub * KEY_BLOCK
        return key_off < lax.broadcasted_iota(jnp.int32, (KEY_BLOCK, tq - q0), 1) + q0

    def key_step(j, diagonal):
        n_sub = ATT_KB // KEY_BLOCK
        first_q = [sub * KEY_BLOCK if diagonal else 0 for sub in range(n_sub)]

        def scores(h):
            out = [None] * n_sub
            for sub in range(n_sub - 1, -1, -1):
                start = pl.multiple_of(j * ATT_KB + sub * KEY_BLOCK, KEY_BLOCK)
                kblk = kp_ref[0, pl.ds(start, KEY_BLOCK), LANES * (h // 2):LANES * (h // 2 + 1)]
                out[sub] = _dot_nt(kblk, qm[h][first_q[sub]:, :])
            return out

        z = {h: scores(h) for h in range(min(ATT_CHAIN, n_heads))}
        for h in range(n_heads):
            a_parts = [None] * n_sub
            for sub in range(n_sub - 1, -1, -1):
                q0 = first_q[sub]
                mask = causal_mask(sub, q0) if diagonal else None
                carry = carry_ref[h]
                a, new_carry = _sb_block(z[h][sub], carry[:, q0:], mask)
                if q0:
                    a = jnp.concatenate([jnp.zeros((KEY_BLOCK, q0), a.dtype), a], axis=1)
                    new_carry = jnp.concatenate([carry[:, :q0], new_carry], axis=1)
                a_parts[sub], carry_ref[h] = a, new_carry
            if h + ATT_CHAIN < n_heads:
                z[h + ATT_CHAIN] = scores(h + ATT_CHAIN)
            vt = vt_ref[0, SB_HEAD_DIM * h:SB_HEAD_DIM * (h + 1),
                        pl.ds(pl.multiple_of(j * ATT_KB, ATT_KB), ATT_KB)]
            acc_ref[SB_HEAD_DIM * h:SB_HEAD_DIM * (h + 1), :] += _dot(vt, jnp.concatenate(a_parts, axis=0))

    def any_alive():
        return jnp.min(carry_ref[...]) < LOG2_DEAD

    key_step(qi, True)

    def body(state):
        j, _ = state
        key_step(j, False)
        return j - 1, any_alive()

    lax.while_loop(lambda st: jnp.logical_and(st[0] >= 0, st[1]), body, (qi - 1, any_alive()))
    o_ref[0] = acc_ref[...].T.astype(o_ref.dtype)


def _sb_attention(q, kp, vt):
    B, S, D = q.shape
    tq = min(ATT_TQ, S)
    assert tq == ATT_KB and S % tq == 0 and D % ATT_LANES == 0
    n_heads = ATT_LANES // SB_HEAD_DIM
    return pl.pallas_call(
        functools.partial(_attn_kernel, tq=tq, n_heads=n_heads),
        grid=(B, D // ATT_LANES, S // tq),
        in_specs=[
            pl.BlockSpec((1, tq, ATT_LANES), lambda b, h, i: (b, i, h)),
            pl.BlockSpec((1, S, ATT_LANES), lambda b, h, i: (b, 0, h)),
            pl.BlockSpec((1, ATT_LANES, S), lambda b, h, i: (b, h, 0)),
        ],
        out_specs=pl.BlockSpec((1, tq, ATT_LANES), lambda b, h, i: (b, i, h)),
        out_shape=jax.ShapeDtypeStruct((B, S, D), _BF16),
        scratch_shapes=[
            pltpu.VMEM((ATT_LANES, tq), _F32),
            pltpu.VMEM((n_heads, SUBLANES, tq), _F32),
        ],
        compiler_params=pltpu.CompilerParams(
            dimension_semantics=("parallel", "parallel", "parallel"),
            vmem_limit_bytes=VMEM_LIMIT_BYTES),
        name="sb_attention",
    )(q, kp, vt)


def _mix_out_kernel(o_ref, h_ref, w_ref, g_ref, b_ref, out_ref, *, tm, rt, alpha):
    def proj(i):
        return _dot(o_ref[0, i * rt:(i + 1) * rt, :], w_ref[...])

    m = proj(0)
    for i in range(tm // rt):
        nxt = proj(i + 1) if (i + 1) * rt < tm else None
        rows = slice(i * rt, (i + 1) * rt)
        out_ref[0, rows, :] = _layer_norm(alpha * h_ref[0, rows, :] + m, g_ref[...], b_ref[...])
        m = nxt


def _mix_out_ln(o, h, w_out, g, b, alpha):
    B, S, D = h.shape
    tm = min(MIX_ROWS, S)
    rt = min(MIX_SUBTILE, tm)
    return pl.pallas_call(
        functools.partial(_mix_out_kernel, tm=tm, rt=rt, alpha=alpha),
        grid=(B, S // tm),
        in_specs=[
            pl.BlockSpec((1, tm, D), lambda b, s: (b, s, 0)),
            pl.BlockSpec((1, tm, D), lambda b, s: (b, s, 0)),
            _resident((D, D), lambda b, s: (0, 0)),
            _resident((1, D), lambda b, s: (0, 0)),
            _resident((1, D), lambda b, s: (0, 0)),
        ],
        out_specs=pl.BlockSpec((1, tm, D), lambda b, s: (b, s, 0)),
        out_shape=jax.ShapeDtypeStruct((B, S, D), _F32),
        compiler_params=pltpu.CompilerParams(
            dimension_semantics=("parallel", "parallel"), vmem_limit_bytes=VMEM_LIMIT_BYTES),
        name="mix_out_ln",
    )(o, h, w_out, g, b)


def _gelu_tanh(x):
    c = math.sqrt(2.0 / math.pi)
    return 0.5 * x * (1.0 + jnp.tanh(c * (x + 0.044715 * (x * x * x))))


def _gmlp_kernel(x_ref, win_ref, lng_ref, lnb_ref, ws_ref, bs_ref, wout_ref, g_ref, b_ref,
                 out_ref, *, tm, rt, width, alpha):
    gw = width // GMLP_GROUPS
    tri = (lax.broadcasted_iota(jnp.int32, (GMLP_CHUNK, GMLP_CHUNK), 0)
           >= lax.broadcasted_iota(jnp.int32, (GMLP_CHUNK, GMLP_CHUNK), 1))
    w_causal = [jnp.where(tri, ws_ref[g], 0.0).astype(_BF16) for g in range(GMLP_GROUPS)]

    def proj_in(i):
        xb = x_ref[0, i * rt:(i + 1) * rt, :].astype(_BF16)
        return _dot(xb, win_ref[:, :width]), _dot(xb, win_ref[:, width:])

    zz = proj_in(0)
    for i in range(tm // rt):
        nxt = proj_in(i + 1) if (i + 1) * rt < tm else None
        u = _gelu_tanh(zz[0])
        v = _gelu_tanh(zz[1])
        vn = _layer_norm(v, lng_ref[...], lnb_ref[...]).astype(_BF16)
        chunks = []
        for ch in range(rt // GMLP_CHUNK):
            r0 = ch * GMLP_CHUNK
            cols = []
            for g in range(GMLP_GROUPS):
                cols.append(_dot(w_causal[g], vn[r0:r0 + GMLP_CHUNK, g * gw:(g + 1) * gw]))
            chunks.append(jnp.concatenate(cols, axis=1) + bs_ref[...])
        s = jnp.concatenate(chunks, axis=0)
        m = _dot((u * s).astype(_BF16), wout_ref[...])
        rows = slice(i * rt, (i + 1) * rt)
        out_ref[0, rows, :] = _layer_norm(alpha * x_ref[0, rows, :] + m, g_ref[...], b_ref[...])
        zz = nxt


def _gmlp_mixer(h, w_in, ln_g, ln_b, w_s, bs_full, w_out, g, b, alpha):
    B, S, D = h.shape
    width = w_out.shape[0]
    tm = min(GMLP_ROWS, S)
    rt = min(GMLP_SUBTILE, tm)
    const2 = lambda b, s: (0, 0)
    return pl.pallas_call(
        functools.partial(_gmlp_kernel, tm=tm, rt=rt, width=width, alpha=alpha),
        grid=(B, S // tm),
        in_specs=[
            pl.BlockSpec((1, tm, D), lambda b, s: (b, s, 0)),
            _resident((D, 2 * width), const2),
            _resident((1, width), const2),
            _resident((1, width), const2),
            _resident((GMLP_GROUPS, GMLP_CHUNK, GMLP_CHUNK), lambda b, s: (0, 0, 0)),
            _resident((GMLP_CHUNK, width), const2),
            _resident((width, D), const2),
            _resident((1, D), const2),
            _resident((1, D), const2),
        ],
        out_specs=pl.BlockSpec((1, tm, D), lambda b, s: (b, s, 0)),
        out_shape=jax.ShapeDtypeStruct((B, S, D), _F32),
        compiler_params=pltpu.CompilerParams(
            dimension_semantics=("parallel", "parallel"), vmem_limit_bytes=VMEM_LIMIT_BYTES),
        name="gmlp_mixer",
    )(h, w_in, ln_g, ln_b, w_s, bs_full, w_out, g, b)


PACKED_ROWS = 2 * SUBLANES


def _slab_tokens(v, zero):
    u = pltpu.bitcast(v, jnp.uint32)
    tokens = []
    for j in range(u.shape[0] // SUBLANES):
        for l in range(u.shape[1] // LANES):
            tokens.append(u[SUBLANES * j:SUBLANES * (j + 1), LANES * l:LANES * (l + 1)] & zero)
    return tokens


def _tie_to_tokens(x, tokens, k_tile):
    n_slabs = x.shape[0] // PACKED_ROWS
    n_lanes = x.shape[1] // LANES
    pieces = [[x[PACKED_ROWS * i:PACKED_ROWS * (i + 1), LANES * l:LANES * (l + 1)] for l in range(n_lanes)]
              for i in range(n_slabs)]
    n_pos = (x.shape[1] // k_tile) * n_slabs
    for j, tok in enumerate(tokens):
        p = j * n_pos // len(tokens)
        k, i = p // n_slabs, p % n_slabs
        l = k * (k_tile // LANES)
        pieces[i][l] = pieces[i][l] + pltpu.bitcast(tok, _BF16)
    return jnp.concatenate([jnp.concatenate(row, axis=1) for row in pieces], axis=0)


def _ffn_kernel(x_ref, wup_ref, cw_ref, cb_ref, wdown_ref, g_ref, b_ref, zero_ref, out_ref,
                acc_ref, tail_ref, xb_ref, a0_ref, a1_ref, a2_ref, a3_ref,
                gated0_ref, gated1_ref, gated2_ref, gated3_ref, *, tm, rs, n_chunks, alpha):
    a_refs = (a0_ref, a1_ref, a2_ref, a3_ref)
    gated_refs = (gated0_ref, gated1_ref, gated2_ref, gated3_ref)
    n_slabs = tm // rs
    n_items = n_chunks * n_slabs

    @pl.when(pl.program_id(1) == 0)
    def _():
        tail_ref[...] = jnp.zeros_like(tail_ref)

    def cast_slab(r):
        xb_ref[r * rs:(r + 1) * rs, :] = x_ref[0, r * rs:(r + 1) * rs, :].astype(_BF16)

    def item(t):
        return t // n_slabs, pl.multiple_of((t % n_slabs) * rs, rs)

    def up(t, slot, tokens=()):
        c, r0 = item(t)
        xs = xb_ref[pl.ds(r0, rs), :]
        if tokens:
            xs = _tie_to_tokens(xs, tokens, MXU_TILE)
        for half in range(2):
            a_refs[slot][half] = _dot(xs, wup_ref[half * n_chunks + c])

    def conv_gate(t, slot_a, slot_g):
        c, _ = item(t)

        def conv_half(half):
            idx = half * n_chunks + c
            a = a_refs[slot_a][half]
            prev = tail_ref[idx]
            tail_ref[idx] = a[rs - SUBLANES:, :]
            ext = jnp.concatenate([prev, a], axis=0)
            a1 = pltpu.roll(ext, 1, axis=0)[SUBLANES:, :]
            a2 = pltpu.roll(ext, 2, axis=0)[SUBLANES:, :]
            w = cw_ref[idx]
            return cb_ref[idx] + w[0:1, :] * a2 + w[1:2, :] * a1 + w[2:3, :] * a

        gate = conv_half(0)
        val = conv_half(1)
        gated = (gate * jax.nn.sigmoid(gate) * val).astype(_BF16)
        gated_refs[slot_g][...] = gated
        return _slab_tokens(gated, zero_ref[...])

    def down(t, slot):
        c, r0 = item(t)
        acc_ref[pl.ds(r0, rs), :] += _dot(gated_refs[slot][...], wdown_ref[c])

    def stage(t, k, do_down=True, do_up=True):
        tokens = conv_gate(t, k % 4, k % 4)
        if do_up:
            up(t + 2, (k + 2) % 4, tokens)
        if do_down:
            down(t - 1, (k - 1) % 4)

    cast_slab(0)
    up(0, 0)
    if n_slabs > 1:
        cast_slab(1)
    up(1, 1)
    for r in range(2, n_slabs):
        cast_slab(r)
    acc_ref[...] = jnp.zeros_like(acc_ref)
    stage(0, 0, do_down=False)
    stage(1, 1)

    def group(q, carry):
        t = FFN_UNROLL * q + 2
        for k in range(FFN_UNROLL):
            stage(t + k, (2 + k) % 4)
        return carry

    lax.fori_loop(0, (n_items - 4) // FFN_UNROLL, group, 0)
    stage(n_items - 2, (n_items - 2) % 4, do_up=False)
    stage(n_items - 1, (n_items - 1) % 4, do_up=False)
    down(n_items - 1, (n_items - 1) % 4)
    out_ref[0] = _layer_norm(alpha * x_ref[0] + acc_ref[...], g_ref[...], b_ref[...])


def _conv_ffn(h, wup_c, cw_c, cb_c, wdown_c, g, b, alpha):
    B, S, D = h.shape
    n2, _, fc = wup_c.shape
    n_chunks = n2 // 2
    tm = min(FFN_ROWS, S)
    rs = min(FFN_SLAB, tm)
    n_items = n_chunks * (tm // rs)
    assert n_items >= 4 + FFN_UNROLL and (n_items - 4) % FFN_UNROLL == 0
    const2 = lambda b, s: (0, 0)
    const3 = lambda b, s: (0, 0, 0)
    return pl.pallas_call(
        functools.partial(_ffn_kernel, tm=tm, rs=rs, n_chunks=n_chunks, alpha=alpha),
        grid=(B, S // tm),
        in_specs=[
            pl.BlockSpec((1, tm, D), lambda b, s: (b, s, 0)),
            _resident((n2, D, fc), const3),
            _resident((n2, CONV_WIDTH, fc), const3),
            _resident((n2, 1, fc), const3),
            _resident((n_chunks, fc, D), const3),
            _resident((1, D), const2),
            _resident((1, D), const2),
            _resident((SUBLANES, LANES), const2),
        ],
        out_specs=pl.BlockSpec((1, tm, D), lambda b, s: (b, s, 0)),
        out_shape=jax.ShapeDtypeStruct((B, S, D), _F32),
        scratch_shapes=[
            pltpu.VMEM((tm, D), _F32),
            pltpu.VMEM((n2, SUBLANES, fc), _F32),
            pltpu.VMEM((tm, D), _BF16),
            pltpu.VMEM((2, rs, fc), _F32),
            pltpu.VMEM((2, rs, fc), _F32),
            pltpu.VMEM((2, rs, fc), _F32),
            pltpu.VMEM((2, rs, fc), _F32),
            pltpu.VMEM((rs, fc), _BF16),
            pltpu.VMEM((rs, fc), _BF16),
            pltpu.VMEM((rs, fc), _BF16),
            pltpu.VMEM((rs, fc), _BF16),
        ],
        compiler_params=pltpu.CompilerParams(
            dimension_semantics=("parallel", "arbitrary"), vmem_limit_bytes=VMEM_LIMIT_BYTES),
        name="conv_ffn",
    )(h, wup_c, cw_c, cb_c, wdown_c, g, b, jnp.zeros((SUBLANES, LANES), jnp.uint32))


MIX_ROWS = 2048
MIX_SUBTILE = 256
GMLP_ROWS = 1024
GMLP_SUBTILE = 256
FFN_ROWS = 1024
FFN_SLAB = 256
FFN_UNROLL = 20
FFN_CHUNK = 256


def _chunk_cols(w, fc):
    r, c = w.shape
    return jnp.transpose(w.reshape(r, c // fc, fc), (1, 0, 2))


def kernel(x, attn_w_in, attn_w_out, gmlp_w_in, gmlp_ln_g, gmlp_ln_b, gmlp_w_s, gmlp_b_s, gmlp_w_out,
           ffn_w_up, ffn_conv_w, ffn_conv_b, ffn_w_down, ln_mix_g, ln_mix_b, ln_ffn_g, ln_ffn_b):
    B, S, D = x.shape
    depth = ffn_w_up.shape[0]
    alpha = (2 * depth) ** 0.25
    d_ff = ffn_w_down.shape[1]
    width = gmlp_w_out.shape[1]
    row = lambda v: v.reshape(1, -1)

    h = x
    for i in range(depth):
        j = i // 2
        if i % 2 == 0:
            w_in = attn_w_in[j]
            wq = w_in[:, :D].astype(_BF16)
            wk = w_in[:, D:2 * D].astype(_BF16)
            wvt = w_in[:, 2 * D:].T.astype(_BF16)
            q, kp, vt = _qkv_proj(h, wq, wk, wvt)
            o = _sb_attention(q, kp, vt)
            h = _mix_out_ln(o, h, attn_w_out[j].astype(_BF16), row(ln_mix_g[i]), row(ln_mix_b[i]), alpha)
        else:
            bs_full = jnp.repeat(gmlp_b_s[j].T, width // GMLP_GROUPS, axis=1)
            h = _gmlp_mixer(h, gmlp_w_in[j].astype(_BF16), row(gmlp_ln_g[j]), row(gmlp_ln_b[j]),
                            gmlp_w_s[j], bs_full, gmlp_w_out[j].astype(_BF16),
                            row(ln_mix_g[i]), row(ln_mix_b[i]), alpha)
        wup_c = _chunk_cols(ffn_w_up[i].astype(_BF16), FFN_CHUNK)
        cw_c = _chunk_cols(ffn_conv_w[i], FFN_CHUNK)
        cb_c = _chunk_cols(ffn_conv_b[i].reshape(1, -1), FFN_CHUNK)
        wdown_c = ffn_w_down[i].astype(_BF16).reshape(d_ff // FFN_CHUNK, FFN_CHUNK, D)
        h = _conv_ffn(h, wup_c, cw_c, cb_c, wdown_c, row(ln_ffn_g[i]), row(ln_ffn_b[i]), alpha)
    return h
```

```python
import functools
import math

import jax
import jax.numpy as jnp
from jax import lax
from jax.experimental import pallas as pl
from jax.experimental.pallas import tpu as pltpu

LN_EPS = 1e-5
CONV_WIDTH = 3
SB_HEAD_DIM = 64
GMLP_GROUPS = 8
GMLP_CHUNK = 128

SUBLANES = 8
LANES = 128
MXU_TILE = 256
KEY_BLOCK = 128
KEY_GROUP = KEY_BLOCK // SUBLANES
VMEM_LIMIT_BYTES = 56 * 1024 * 1024

ATT_TQ = 256
ATT_KB = 256
ATT_LANES = 1024
ATT_CHAIN = 4
LOG2E = 1.0 / math.log(2.0)
INV_LN2 = 1.0 / math.log(2.0)
LOG2_DEAD = 150.0

_BF16 = jnp.bfloat16
_F32 = jnp.float32


def _resident(block_shape, index_map):
    return pl.BlockSpec(block_shape, index_map, pipeline_mode=pl.Buffered(1))


def _layer_norm(y, g, b):
    mu = jnp.mean(y, axis=-1, keepdims=True)
    d = y - mu
    var = jnp.mean(d * d, axis=-1, keepdims=True)
    return d * lax.rsqrt(var + LN_EPS) * g + b


def _dot(a, b):
    return jnp.dot(a, b, preferred_element_type=_F32)


def _dot_nt(a, b):
    return lax.dot_general(a, b, (((1,), (1,)), ((), ())), preferred_element_type=_F32)


def _qkv_kernel(h_ref, wq_ref, wk_ref, wvt_ref, q_ref, kp_ref, vt_ref, *, tm, scale):
    hb = h_ref[0].astype(_BF16)
    row = lax.broadcasted_iota(jnp.int32, (KEY_BLOCK, KEY_BLOCK), 0)
    col = lax.broadcasted_iota(jnp.int32, (KEY_BLOCK, KEY_BLOCK), 1)
    perm = (col == (row % SUBLANES) * KEY_GROUP + row // SUBLANES).astype(_BF16)
    hp = jnp.concatenate(
        [_dot(perm, hb[blk * KEY_BLOCK:(blk + 1) * KEY_BLOCK, :]) for blk in range(tm // KEY_BLOCK)],
        axis=0).astype(_BF16)
    q_ref[0] = (_dot(hb, wq_ref[...]) * scale).astype(_BF16)
    kp_ref[0] = _dot(hp, wk_ref[...]).astype(_BF16)
    vt_ref[0] = _dot_nt(wvt_ref[...], hp).astype(_BF16)


def _qkv_proj(h, wq, wk, wvt):
    B, S, D = h.shape
    tm = min(512, S)
    scale = SB_HEAD_DIM ** -0.5 * LOG2E
    return pl.pallas_call(
        functools.partial(_qkv_kernel, tm=tm, scale=scale),
        grid=(B, S // tm),
        in_specs=[
            pl.BlockSpec((1, tm, D), lambda b, s: (b, s, 0)),
            _resident((D, D), lambda b, s: (0, 0)),
            _resident((D, D), lambda b, s: (0, 0)),
            _resident((D, D), lambda b, s: (0, 0)),
        ],
        out_specs=[
            pl.BlockSpec((1, tm, D), lambda b, s: (b, s, 0)),
            pl.BlockSpec((1, tm, D), lambda b, s: (b, s, 0)),
            pl.BlockSpec((1, D, tm), lambda b, s: (b, 0, s)),
        ],
        out_shape=[
            jax.ShapeDtypeStruct((B, S, D), _BF16),
            jax.ShapeDtypeStruct((B, S, D), _BF16),
            jax.ShapeDtypeStruct((B, D, S), _BF16),
        ],
        compiler_params=pltpu.CompilerParams(
            dimension_semantics=("parallel", "parallel"), vmem_limit_bytes=VMEM_LIMIT_BYTES),
        name="qkv_proj",
    )(h, wq, wk, wvt)


def _sublane_suffix_scan(g):
    row = lax.broadcasted_iota(jnp.int32, g.shape, 0)
    x = g
    for sh in (1, 2, 4):
        shifted = pltpu.roll(x, SUBLANES - sh, axis=0)
        x = x + jnp.where(row + sh < SUBLANES, shifted, 0.0)
    return x


def _sb_block(z, carry, causal):
    m = jnp.maximum(z, 0.0) + jnp.log(1.0 + jnp.exp2(-jnp.abs(z))) * INV_LN2
    if causal is not None:
        m = jnp.where(causal, m, 0.0)
    tiles = [m[SUBLANES * v:SUBLANES * (v + 1), :] for v in range(KEY_GROUP)]
    run = [None] * KEY_GROUP
    run[KEY_GROUP - 1] = tiles[KEY_GROUP - 1]
    for v in range(KEY_GROUP - 2, -1, -1):
        run[v] = run[v + 1] + tiles[v]
    group_tot = run[0]
    incl = _sublane_suffix_scan(group_tot)
    base = (incl - group_tot) + carry
    total = jnp.concatenate([run[v] + base for v in range(KEY_GROUP)], axis=0)
    a = jnp.exp2(z - total)
    if causal is not None:
        a = jnp.where(causal, a, 0.0)
    new_carry = carry + jnp.broadcast_to(incl[0:1, :], carry.shape)
    return a.astype(_BF16), new_carry


def _attn_kernel(q_ref, kp_ref, vt_ref, o_ref, acc_ref, carry_ref, *, tq, n_heads):
    qi = pl.program_id(2)
    lane = lax.broadcasted_iota(jnp.int32, (tq, LANES), 1)
    qm = []
    for h in range(n_heads):
        qpair = q_ref[0, :, LANES * (h // 2):LANES * (h // 2 + 1)]
        qm.append(jnp.where((lane // SB_HEAD_DIM) == h % 2, qpair, jnp.zeros_like(qpair)))

    acc_ref[...] = jnp.zeros_like(acc_ref)
    carry_ref[...] = jnp.zeros_like(carry_ref)

    def causal_mask(sub, q0):
        row = lax.broadcasted_iota(jnp.int32, (KEY_BLOCK, tq - q0), 0)
        key_off = (row % SUBLANES) * KEY_GROUP + row // SUBLANES + sub * KEY_BLOCK
        return key_off < lax.broadcasted_iota(jnp.int32, (KEY_BLOCK, tq - q0), 1) + q0

    def key_step(j, diagonal):
        n_sub = ATT_KB // KEY_BLOCK
        first_q = [sub * KEY_BLOCK if diagonal else 0 for sub in range(n_sub)]

        def scores(h):
            out = [None] * n_sub
            for sub in range(n_sub - 1, -1, -1):
                start = pl.multiple_of(j * ATT_KB + sub * KEY_BLOCK, KEY_BLOCK)
                kblk = kp_ref[0, pl.ds(start, KEY_BLOCK), LANES * (h // 2):LANES * (h // 2 + 1)]
                out[sub] = _dot_nt(kblk, qm[h][first_q[sub]:, :])
            return out

        z = {h: scores(h) for h in range(min(ATT_CHAIN, n_heads))}
        for h in range(n_heads):
            a_parts = [None] * n_sub
            for sub in range(n_sub - 1, -1, -1):
                q0 = first_q[sub]
                mask = causal_mask(sub, q0) if diagonal else None
                carry = carry_ref[h]
                a, new_carry = _sb_block(z[h][sub], carry[:, q0:], mask)
                if q0:
                    a = jnp.concatenate([jnp.zeros((KEY_BLOCK, q0), a.dtype), a], axis=1)
                    new_carry = jnp.concatenate([carry[:, :q0], new_carry], axis=1)
                a_parts[sub], carry_ref[h] = a, new_carry
            if h + ATT_CHAIN < n_heads:
                z[h + ATT_CHAIN] = scores(h + ATT_CHAIN)
            vt = vt_ref[0, SB_HEAD_DIM * h:SB_HEAD_DIM * (h + 1),
                        pl.ds(pl.multiple_of(j * ATT_KB, ATT_KB), ATT_KB)]
            acc_ref[SB_HEAD_DIM * h:SB_HEAD_DIM * (h + 1), :] += _dot(vt, jnp.concatenate(a_parts, axis=0))

    def any_alive():
        return jnp.min(carry_ref[...]) < LOG2_DEAD

    key_step(qi, True)

    def body(state):
        j, _ = state
        key_step(j, False)
        return j - 1, any_alive()

    lax.while_loop(lambda st: jnp.logical_and(st[0] >= 0, st[1]), body, (qi - 1, any_alive()))
    o_ref[0] = acc_ref[...].T.astype(o_ref.dtype)


def _sb_attention(q, kp, vt):
    B, S, D = q.shape
    tq = min(ATT_TQ, S)
    assert tq == ATT_KB and S % tq == 0 and D % ATT_LANES == 0
    n_heads = ATT_LANES // SB_HEAD_DIM
    return pl.pallas_call(
        functools.partial(_attn_kernel, tq=tq, n_heads=n_heads),
        grid=(B, D // ATT_LANES, S // tq),
        in_specs=[
            pl.BlockSpec((1, tq, ATT_LANES), lambda b, h, i: (b, i, h)),
            pl.BlockSpec((1, S, ATT_LANES), lambda b, h, i: (b, 0, h)),
            pl.BlockSpec((1, ATT_LANES, S), lambda b, h, i: (b, h, 0)),
        ],
        out_specs=pl.BlockSpec((1, tq, ATT_LANES), lambda b, h, i: (b, i, h)),
        out_shape=jax.ShapeDtypeStruct((B, S, D), _BF16),
        scratch_shapes=[
            pltpu.VMEM((ATT_LANES, tq), _F32),
            pltpu.VMEM((n_heads, SUBLANES, tq), _F32),
        ],
        compiler_params=pltpu.CompilerParams(
            dimension_semantics=("parallel", "parallel", "parallel"),
            vmem_limit_bytes=VMEM_LIMIT_BYTES),
        name="sb_attention",
    )(q, kp, vt)


def _mix_out_kernel(o_ref, h_ref, w_ref, g_ref, b_ref, out_ref, *, tm, rt, alpha):
    def proj(i):
        return _dot(o_ref[0, i * rt:(i + 1) * rt, :], w_ref[...])

    m = proj(0)
    for i in range(tm // rt):
        nxt = proj(i + 1) if (i + 1) * rt < tm else None
        rows = slice(i * rt, (i + 1) * rt)
        out_ref[0, rows, :] = _layer_norm(alpha * h_ref[0, rows, :] + m, g_ref[...], b_ref[...])
        m = nxt


def _mix_out_ln(o, h, w_out, g, b, alpha):
    B, S, D = h.shape
    tm = min(MIX_ROWS, S)
    rt = min(MIX_SUBTILE, tm)
    return pl.pallas_call(
        functools.partial(_mix_out_kernel, tm=tm, rt=rt, alpha=alpha),
        grid=(B, S // tm),
        in_specs=[
            pl.BlockSpec((1, tm, D), lambda b, s: (b, s, 0)),
            pl.BlockSpec((1, tm, D), lambda b, s: (b, s, 0)),
            _resident((D, D), lambda b, s: (0, 0)),
            _resident((1, D), lambda b, s: (0, 0)),
            _resident((1, D), lambda b, s: (0, 0)),
        ],
        out_specs=pl.BlockSpec((1, tm, D), lambda b, s: (b, s, 0)),
        out_shape=jax.ShapeDtypeStruct((B, S, D), _F32),
        compiler_params=pltpu.CompilerParams(
            dimension_semantics=("parallel", "parallel"), vmem_limit_bytes=VMEM_LIMIT_BYTES),
        name="mix_out_ln",
    )(o, h, w_out, g, b)


def _gelu_tanh(x):
    c = math.sqrt(2.0 / math.pi)
    return 0.5 * x * (1.0 + jnp.tanh(c * (x + 0.044715 * (x * x * x))))


def _gmlp_kernel(x_ref, win_ref, lng_ref, lnb_ref, ws_ref, bs_ref, wout_ref, g_ref, b_ref,
                 out_ref, *, tm, rt, width, alpha):
    gw = width // GMLP_GROUPS
    tri = (lax.broadcasted_iota(jnp.int32, (GMLP_CHUNK, GMLP_CHUNK), 0)
           >= lax.broadcasted_iota(jnp.int32, (GMLP_CHUNK, GMLP_CHUNK), 1))
    w_causal = [jnp.where(tri, ws_ref[g], 0.0).astype(_BF16) for g in range(GMLP_GROUPS)]

    def proj_in(i):
        xb = x_ref[0, i * rt:(i + 1) * rt, :].astype(_BF16)
        return _dot(xb, win_ref[:, :width]), _dot(xb, win_ref[:, width:])

    zz = proj_in(0)
    for i in range(tm // rt):
        nxt = proj_in(i + 1) if (i + 1) * rt < tm else None
        u = _gelu_tanh(zz[0])
        v = _gelu_tanh(zz[1])
        vn = _layer_norm(v, lng_ref[...], lnb_ref[...]).astype(_BF16)
        chunks = []
        for ch in range(rt // GMLP_CHUNK):
            r0 = ch * GMLP_CHUNK
            cols = []
            for g in range(GMLP_GROUPS):
                cols.append(_dot(w_causal[g], vn[r0:r0 + GMLP_CHUNK, g * gw:(g + 1) * gw]))
            chunks.append(jnp.concatenate(cols, axis=1) + bs_ref[...])
        s = jnp.concatenate(chunks, axis=0)
        m = _dot((u * s).astype(_BF16), wout_ref[...])
        rows = slice(i * rt, (i + 1) * rt)
        out_ref[0, rows, :] = _layer_norm(alpha * x_ref[0, rows, :] + m, g_ref[...], b_ref[...])
        zz = nxt


def _gmlp_mixer(h, w_in, ln_g, ln_b, w_s, bs_full, w_out, g, b, alpha):
    B, S, D = h.shape
    width = w_out.shape[0]
    tm = min(GMLP_ROWS, S)
    rt = min(GMLP_SUBTILE, tm)
    const2 = lambda b, s: (0, 0)
    return pl.pallas_call(
        functools.partial(_gmlp_kernel, tm=tm, rt=rt, width=width, alpha=alpha),
        grid=(B, S // tm),
        in_specs=[
            pl.BlockSpec((1, tm, D), lambda b, s: (b, s, 0)),
            _resident((D, 2 * width), const2),
            _resident((1, width), const2),
            _resident((1, width), const2),
            _resident((GMLP_GROUPS, GMLP_CHUNK, GMLP_CHUNK), lambda b, s: (0, 0, 0)),
            _resident((GMLP_CHUNK, width), const2),
            _resident((width, D), const2),
            _resident((1, D), const2),
            _resident((1, D), const2),
        ],
        out_specs=pl.BlockSpec((1, tm, D), lambda b, s: (b, s, 0)),
        out_shape=jax.ShapeDtypeStruct((B, S, D), _F32),
        compiler_params=pltpu.CompilerParams(
            dimension_semantics=("parallel", "parallel"), vmem_limit_bytes=VMEM_LIMIT_BYTES),
        name="gmlp_mixer",
    )(h, w_in, ln_g, ln_b, w_s, bs_full, w_out, g, b)


PACKED_ROWS = 2 * SUBLANES


def _slab_tokens(v, zero):
    u = pltpu.bitcast(v, jnp.uint32)
    tokens = []
    for j in range(u.shape[0] // SUBLANES):
        for l in range(u.shape[1] // LANES):
            tokens.append(u[SUBLANES * j:SUBLANES * (j + 1), LANES * l:LANES * (l + 1)] & zero)
    return tokens


def _tie_to_tokens(x, tokens, k_tile):
    n_slabs = x.shape[0] // PACKED_ROWS
    n_lanes = x.shape[1] // LANES
    pieces = [[x[PACKED_ROWS * i:PACKED_ROWS * (i + 1), LANES * l:LANES * (l + 1)] for l in range(n_lanes)]
              for i in range(n_slabs)]
    n_pos = (x.shape[1] // k_tile) * n_slabs
    for j, tok in enumerate(tokens):
        p = j * n_pos // len(tokens)
        k, i = p // n_slabs, p % n_slabs
        l = k * (k_tile // LANES)
        pieces[i][l] = pieces[i][l] + pltpu.bitcast(tok, _BF16)
    return jnp.concatenate([jnp.concatenate(row, axis=1) for row in pieces], axis=0)


def _ffn_kernel(x_ref, wup_ref, cw_ref, cb_ref, wdown_ref, g_ref, b_ref, zero_ref, out_ref,
                acc_ref, tail_ref, xb_ref, a0_ref, a1_ref, a2_ref, a3_ref,
                gated0_ref, gated1_ref, gated2_ref, gated3_ref, *, tm, rs, n_chunks, alpha):
    a_refs = (a0_ref, a1_ref, a2_ref, a3_ref)
    gated_refs = (gated0_ref, gated1_ref, gated2_ref, gated3_ref)
    n_slabs = tm // rs
    n_items = n_chunks * n_slabs

    @pl.when(pl.program_id(1) == 0)
    def _():
        tail_ref[...] = jnp.zeros_like(tail_ref)

    def cast_slab(r):
        xb_ref[r * rs:(r + 1) * rs, :] = x_ref[0, r * rs:(r + 1) * rs, :].astype(_BF16)

    def item(t):
        return t // n_slabs, pl.multiple_of((t % n_slabs) * rs, rs)

    def up(t, slot, tokens=()):
        c, r0 = item(t)
        xs = xb_ref[pl.ds(r0, rs), :]
        if tokens:
            xs = _tie_to_tokens(xs, tokens, MXU_TILE)
        for half in range(2):
            a_refs[slot][half] = _dot(xs, wup_ref[half * n_chunks + c])

    def conv_gate(t, slot_a, slot_g):
        c, _ = item(t)

        def conv_half(half):
            idx = half * n_chunks + c
            a = a_refs[slot_a][half]
            prev = tail_ref[idx]
            tail_ref[idx] = a[rs - SUBLANES:, :]
            ext = jnp.concatenate([prev, a], axis=0)
            a1 = pltpu.roll(ext, 1, axis=0)[SUBLANES:, :]
            a2 = pltpu.roll(ext, 2, axis=0)[SUBLANES:, :]
            w = cw_ref[idx]
            return cb_ref[idx] + w[0:1, :] * a2 + w[1:2, :] * a1 + w[2:3, :] * a

        gate = conv_half(0)
        val = conv_half(1)
        gated = (gate * jax.nn.sigmoid(gate) * val).astype(_BF16)
        gated_refs[slot_g][...] = gated
        return _slab_tokens(gated, zero_ref[...])

    def down(t, slot):
        c, r0 = item(t)
        acc_ref[pl.ds(r0, rs), :] += _dot(gated_refs[slot][...], wdown_ref[c])

    def stage(t, k, do_down=True, do_up=True):
        tokens = conv_gate(t, k % 4, k % 4)
        if do_up:
            up(t + 2, (k + 2) % 4, tokens)
        if do_down:
            down(t - 1, (k - 1) % 4)

    cast_slab(0)
    up(0, 0)
    if n_slabs > 1:
        cast_slab(1)
    up(1, 1)
    for r in range(2, n_slabs):
        cast_slab(r)
    acc_ref[...] = jnp.zeros_like(acc_ref)
    stage(0, 0, do_down=False)
    stage(1, 1)

    def group(q, carry):
        t = FFN_UNROLL * q + 2
        for k in range(FFN_UNROLL):
            stage(t + k, (2 + k) % 4)
        return carry

    lax.fori_loop(0, (n_items - 4) // FFN_UNROLL, group, 0)
    stage(n_items - 2, (n_items - 2) % 4, do_up=False)
    stage(n_items - 1, (n_items - 1) % 4, do_up=False)
    down(n_items - 1, (n_items - 1) % 4)
    out_ref[0] = _layer_norm(alpha * x_ref[0] + acc_ref[...], g_ref[...], b_ref[...])


def _conv_ffn(h, wup_c, cw_c, cb_c, wdown_c, g, b, alpha):
    B, S, D = h.shape
    n2, _, fc = wup_c.shape
    n_chunks = n2 // 2
    tm = min(FFN_ROWS, S)
    rs = min(FFN_SLAB, tm)
    n_items = n_chunks * (tm // rs)
    assert n_items >= 4 + FFN_UNROLL and (n_items - 4) % FFN_UNROLL == 0
    const2 = lambda b, s: (0, 0)
    const3 = lambda b, s: (0, 0, 0)
    return pl.pallas_call(
        functools.partial(_ffn_kernel, tm=tm, rs=rs, n_chunks=n_chunks, alpha=alpha),
        grid=(B, S // tm),
        in_specs=[
            pl.BlockSpec((1, tm, D), lambda b, s: (b, s, 0)),
            _resident((n2, D, fc), const3),
            _resident((n2, CONV_WIDTH, fc), const3),
            _resident((n2, 1, fc), const3),
            _resident((n_chunks, fc, D), const3),
            _resident((1, D), const2),
            _resident((1, D), const2),
            _resident((SUBLANES, LANES), const2),
        ],
        out_specs=pl.BlockSpec((1, tm, D), lambda b, s: (b, s, 0)),
        out_shape=jax.ShapeDtypeStruct((B, S, D), _F32),
        scratch_shapes=[
            pltpu.VMEM((tm, D), _F32),
            pltpu.VMEM((n2, SUBLANES, fc), _F32),
            pltpu.VMEM((tm, D), _BF16),
            pltpu.VMEM((2, rs, fc), _F32),
            pltpu.VMEM((2, rs, fc), _F32),
            pltpu.VMEM((2, rs, fc), _F32),
            pltpu.VMEM((2, rs, fc), _F32),
            pltpu.VMEM((rs, fc), _BF16),
            pltpu.VMEM((rs, fc), _BF16),
            pltpu.VMEM((rs, fc), _BF16),
            pltpu.VMEM((rs, fc), _BF16),
        ],
        compiler_params=pltpu.CompilerParams(
            dimension_semantics=("parallel", "arbitrary"), vmem_limit_bytes=VMEM_LIMIT_BYTES),
        name="conv_ffn",
    )(h, wup_c, cw_c, cb_c, wdown_c, g, b, jnp.zeros((SUBLANES, LANES), jnp.uint32))


MIX_ROWS = 2048
MIX_SUBTILE = 256
GMLP_ROWS = 512
GMLP_SUBTILE = 256
FFN_ROWS = 1024
FFN_SLAB = 256
FFN_UNROLL = 20
FFN_CHUNK = 256


def _chunk_cols(w, fc):
    r, c = w.shape
    return jnp.transpose(w.reshape(r, c // fc, fc), (1, 0, 2))


def kernel(x, attn_w_in, attn_w_out, gmlp_w_in, gmlp_ln_g, gmlp_ln_b, gmlp_w_s, gmlp_b_s, gmlp_w_out,
           ffn_w_up, ffn_conv_w, ffn_conv_b, ffn_w_down, ln_mix_g, ln_mix_b, ln_ffn_g, ln_ffn_b):
    B, S, D = x.shape
    depth = ffn_w_up.shape[0]
    alpha = (2 * depth) ** 0.25
    d_ff = ffn_w_down.shape[1]
    width = gmlp_w_out.shape[1]
    row = lambda v: v.reshape(1, -1)

    h = x
    for i in range(depth):
        j = i // 2
        if i % 2 == 0:
            w_in = attn_w_in[j]
            wq = w_in[:, :D].astype(_BF16)
            wk = w_in[:, D:2 * D].astype(_BF16)
            wvt = w_in[:, 2 * D:].T.astype(_BF16)
            q, kp, vt = _qkv_proj(h, wq, wk, wvt)
            o = _sb_attention(q, kp, vt)
            h = _mix_out_ln(o, h, attn_w_out[j].astype(_BF16), row(ln_mix_g[i]), row(ln_mix_b[i]), alpha)
        else:
            bs_full = jnp.repeat(gmlp_b_s[j].T, width // GMLP_GROUPS, axis=1)
            h = _gmlp_mixer(h, gmlp_w_in[j].astype(_BF16), row(gmlp_ln_g[j]), row(gmlp_ln_b[j]),
                            gmlp_w_s[j], bs_full, gmlp_w_out[j].astype(_BF16),
                            row(ln_mix_g[i]), row(ln_mix_b[i]), alpha)
        wup_c = _chunk_cols(ffn_w_up[i].astype(_BF16), FFN_CHUNK)
        cw_c = _chunk_cols(ffn_conv_w[i], FFN_CHUNK)
        cb_c = _chunk_cols(ffn_conv_b[i].reshape(1, -1), FFN_CHUNK)
        wdown_c = ffn_w_down[i].astype(_BF16).reshape(d_ff // FFN_CHUNK, FFN_CHUNK, D)
        h = _conv_ffn(h, wup_c, cw_c, cb_c, wdown_c, row(ln_ffn_g[i]), row(ln_ffn_b[i]), alpha)
    return h
```

```python
import functools
import math

import jax
import jax.numpy as jnp
from jax import lax
from jax.experimental import pallas as pl
from jax.experimental.pallas import tpu as pltpu

LN_EPS = 1e-5
CONV_WIDTH = 3
SB_HEAD_DIM = 64
GMLP_GROUPS = 8
GMLP_CHUNK = 128

SUBLANES = 8
LANES = 128
MXU_TILE = 256
KEY_BLOCK = 128
KEY_GROUP = KEY_BLOCK // SUBLANES
VMEM_LIMIT_BYTES = 56 * 1024 * 1024

ATT_TQ = 256
ATT_KB = 256
ATT_LANES = 1024
ATT_CHAIN = 4
LOG2E = 1.0 / math.log(2.0)
INV_LN2 = 1.0 / math.log(2.0)
LOG2_DEAD = 150.0

_BF16 = jnp.bfloat16
_F32 = jnp.float32


def _resident(block_shape, index_map):
    return pl.BlockSpec(block_shape, index_map, pipeline_mode=pl.Buffered(1))


def _layer_norm(y, g, b):
    mu = jnp.mean(y, axis=-1, keepdims=True)
    d = y - mu
    var = jnp.mean(d * d, axis=-1, keepdims=True)
    return d * lax.rsqrt(var + LN_EPS) * g + b


def _dot(a, b):
    return jnp.dot(a, b, preferred_element_type=_F32)


def _dot_nt(a, b):
    return lax.dot_general(a, b, (((1,), (1,)), ((), ())), preferred_element_type=_F32)


def _qkv_kernel(h_ref, wq_ref, wk_ref, wvt_ref, q_ref, kp_ref, vt_ref, *, tm, rt, scale):
    row = lax.broadcasted_iota(jnp.int32, (KEY_BLOCK, KEY_BLOCK), 0)
    col = lax.broadcasted_iota(jnp.int32, (KEY_BLOCK, KEY_BLOCK), 1)
    perm = (col == (row % SUBLANES) * KEY_GROUP + row // SUBLANES).astype(_BF16)

    def project(i):
        hb = h_ref[0, i * rt:(i + 1) * rt, :].astype(_BF16)
        hp = jnp.concatenate(
            [_dot(perm, hb[blk * KEY_BLOCK:(blk + 1) * KEY_BLOCK, :]) for blk in range(rt // KEY_BLOCK)],
            axis=0).astype(_BF16)
        return _dot(hb, wq_ref[...]), _dot(hp, wk_ref[...]), _dot_nt(wvt_ref[...], hp)

    cur = project(0)
    for i in range(tm // rt):
        nxt = project(i + 1) if (i + 1) * rt < tm else None
        rows = slice(i * rt, (i + 1) * rt)
        q_ref[0, rows, :] = (cur[0] * scale).astype(_BF16)
        kp_ref[0, rows, :] = cur[1].astype(_BF16)
        vt_ref[0, :, rows] = cur[2].astype(_BF16)
        cur = nxt


def _qkv_proj(h, wq, wk, wvt):
    B, S, D = h.shape
    tm = min(QKV_ROWS, S)
    rt = min(QKV_SUBTILE, tm)
    scale = SB_HEAD_DIM ** -0.5 * LOG2E
    return pl.pallas_call(
        functools.partial(_qkv_kernel, tm=tm, rt=rt, scale=scale),
        grid=(B, S // tm),
        in_specs=[
            pl.BlockSpec((1, tm, D), lambda b, s: (b, s, 0)),
            _resident((D, D), lambda b, s: (0, 0)),
            _resident((D, D), lambda b, s: (0, 0)),
            _resident((D, D), lambda b, s: (0, 0)),
        ],
        out_specs=[
            pl.BlockSpec((1, tm, D), lambda b, s: (b, s, 0)),
            pl.BlockSpec((1, tm, D), lambda b, s: (b, s, 0)),
            pl.BlockSpec((1, D, tm), lambda b, s: (b, 0, s)),
        ],
        out_shape=[
            jax.ShapeDtypeStruct((B, S, D), _BF16),
            jax.ShapeDtypeStruct((B, S, D), _BF16),
            jax.ShapeDtypeStruct((B, D, S), _BF16),
        ],
        compiler_params=pltpu.CompilerParams(
            dimension_semantics=("parallel", "parallel"), vmem_limit_bytes=VMEM_LIMIT_BYTES),
        name="qkv_proj",
    )(h, wq, wk, wvt)


def _sublane_suffix_scan(g):
    row = lax.broadcasted_iota(jnp.int32, g.shape, 0)
    x = g
    for sh in (1, 2, 4):
        shifted = pltpu.roll(x, SUBLANES - sh, axis=0)
        x = x + jnp.where(row + sh < SUBLANES, shifted, 0.0)
    return x


def _sb_block(z, carry, causal):
    m = jnp.maximum(z, 0.0) + jnp.log(1.0 + jnp.exp2(-jnp.abs(z))) * INV_LN2
    if causal is not None:
        m = jnp.where(causal, m, 0.0)
    tiles = [m[SUBLANES * v:SUBLANES * (v + 1), :] for v in range(KEY_GROUP)]
    run = [None] * KEY_GROUP
    run[KEY_GROUP - 1] = tiles[KEY_GROUP - 1]
    for v in range(KEY_GROUP - 2, -1, -1):
        run[v] = run[v + 1] + tiles[v]
    group_tot = run[0]
    incl = _sublane_suffix_scan(group_tot)
    base = (incl - group_tot) + carry
    total = jnp.concatenate([run[v] + base for v in range(KEY_GROUP)], axis=0)
    a = jnp.exp2(z - total)
    if causal is not None:
        a = jnp.where(causal, a, 0.0)
    new_carry = carry + jnp.broadcast_to(incl[0:1, :], carry.shape)
    return a.astype(_BF16), new_carry


def _attn_kernel(q_ref, kp_ref, vt_ref, o_ref, acc_ref, carry_ref, *, tq, n_heads):
    qi = pl.program_id(2)
    lane = lax.broadcasted_iota(jnp.int32, (tq, LANES), 1)
    qm = []
    for h in range(n_heads):
        qpair = q_ref[0, :, LANES * (h // 2):LANES * (h // 2 + 1)]
        qm.append(jnp.where((lane // SB_HEAD_DIM) == h % 2, qpair, jnp.zeros_like(qpair)))

    acc_ref[...] = jnp.zeros_like(acc_ref)
    carry_ref[...] = jnp.zeros_like(carry_ref)

    def causal_mask(sub, q0):
        row = lax.broadcasted_iota(jnp.int32, (KEY_BLOCK, tq - q0), 0)
        key_off = (row % SUBLANES) * KEY_GROUP + row // SUBLANES + sub * KEY_BLOCK
        return key_off < lax.broadcasted_iota(jnp.int32, (KEY_BLOCK, tq - q0), 1) + q0

    def key_step(j, diagonal):
        n_sub = ATT_KB // KEY_BLOCK
        first_q = [sub * KEY_BLOCK if diagonal else 0 for sub in range(n_sub)]

        def scores(h):
            out = [None] * n_sub
            for sub in range(n_sub - 1, -1, -1):
                start = pl.multiple_of(j * ATT_KB + sub * KEY_BLOCK, KEY_BLOCK)
                kblk = kp_ref[0, pl.ds(start, KEY_BLOCK), LANES * (h // 2):LANES * (h // 2 + 1)]
                out[sub] = _dot_nt(kblk, qm[h][first_q[sub]:, :])
            return out

        z = {h: scores(h) for h in range(min(ATT_CHAIN, n_heads))}
        for h in range(n_heads):
            a_parts = [None] * n_sub
            for sub in range(n_sub - 1, -1, -1):
                q0 = first_q[sub]
                mask = causal_mask(sub, q0) if diagonal else None
                carry = carry_ref[h]
                a, new_carry = _sb_block(z[h][sub], carry[:, q0:], mask)
                if q0:
                    a = jnp.concatenate([jnp.zeros((KEY_BLOCK, q0), a.dtype), a], axis=1)
                    new_carry = jnp.concatenate([carry[:, :q0], new_carry], axis=1)
                a_parts[sub], carry_ref[h] = a, new_carry
            if h + ATT_CHAIN < n_heads:
                z[h + ATT_CHAIN] = scores(h + ATT_CHAIN)
            vt = vt_ref[0, SB_HEAD_DIM * h:SB_HEAD_DIM * (h + 1),
                        pl.ds(pl.multiple_of(j * ATT_KB, ATT_KB), ATT_KB)]
            acc_ref[SB_HEAD_DIM * h:SB_HEAD_DIM * (h + 1), :] += _dot(vt, jnp.concatenate(a_parts, axis=0))

    def any_alive():
        return jnp.min(carry_ref[...]) < LOG2_DEAD

    key_step(qi, True)

    def body(state):
        j, _ = state
        key_step(j, False)
        return j - 1, any_alive()

    lax.while_loop(lambda st: jnp.logical_and(st[0] >= 0, st[1]), body, (qi - 1, any_alive()))
    o_ref[0] = acc_ref[...].T.astype(o_ref.dtype)


def _sb_attention(q, kp, vt):
    B, S, D = q.shape
    tq = min(ATT_TQ, S)
    assert tq == ATT_KB and S % tq == 0 and D % ATT_LANES == 0
    n_heads = ATT_LANES // SB_HEAD_DIM
    return pl.pallas_call(
        functools.partial(_attn_kernel, tq=tq, n_heads=n_heads),
        grid=(B, D // ATT_LANES, S // tq),
        in_specs=[
            pl.BlockSpec((1, tq, ATT_LANES), lambda b, h, i: (b, i, h)),
            pl.BlockSpec((1, S, ATT_LANES), lambda b, h, i: (b, 0, h)),
            pl.BlockSpec((1, ATT_LANES, S), lambda b, h, i: (b, h, 0)),
        ],
        out_specs=pl.BlockSpec((1, tq, ATT_LANES), lambda b, h, i: (b, i, h)),
        out_shape=jax.ShapeDtypeStruct((B, S, D), _BF16),
        scratch_shapes=[
            pltpu.VMEM((ATT_LANES, tq), _F32),
            pltpu.VMEM((n_heads, SUBLANES, tq), _F32),
        ],
        compiler_params=pltpu.CompilerParams(
            dimension_semantics=("parallel", "parallel", "parallel"),
            vmem_limit_bytes=VMEM_LIMIT_BYTES),
        name="sb_attention",
    )(q, kp, vt)


def _mix_out_kernel(o_ref, h_ref, w_ref, g_ref, b_ref, out_ref, *, tm, rt, alpha):
    def proj(i):
        return _dot(o_ref[0, i * rt:(i + 1) * rt, :], w_ref[...])

    m = proj(0)
    for i in range(tm // rt):
        nxt = proj(i + 1) if (i + 1) * rt < tm else None
        rows = slice(i * rt, (i + 1) * rt)
        out_ref[0, rows, :] = _layer_norm(alpha * h_ref[0, rows, :] + m, g_ref[...], b_ref[...])
        m = nxt


def _mix_out_ln(o, h, w_out, g, b, alpha):
    B, S, D = h.shape
    tm = min(MIX_ROWS, S)
    rt = min(MIX_SUBTILE, tm)
    return pl.pallas_call(
        functools.partial(_mix_out_kernel, tm=tm, rt=rt, alpha=alpha),
        grid=(B, S // tm),
        in_specs=[
            pl.BlockSpec((1, tm, D), lambda b, s: (b, s, 0)),
            pl.BlockSpec((1, tm, D), lambda b, s: (b, s, 0)),
            _resident((D, D), lambda b, s: (0, 0)),
            _resident((1, D), lambda b, s: (0, 0)),
            _resident((1, D), lambda b, s: (0, 0)),
        ],
        out_specs=pl.BlockSpec((1, tm, D), lambda b, s: (b, s, 0)),
        out_shape=jax.ShapeDtypeStruct((B, S, D), _F32),
        compiler_params=pltpu.CompilerParams(
            dimension_semantics=("parallel", "parallel"), vmem_limit_bytes=VMEM_LIMIT_BYTES),
        name="mix_out_ln",
    )(o, h, w_out, g, b)


def _gelu_tanh(x):
    c = math.sqrt(2.0 / math.pi)
    return 0.5 * x * (1.0 + jnp.tanh(c * (x + 0.044715 * (x * x * x))))


def _gmlp_kernel(x_ref, win_ref, lng_ref, lnb_ref, ws_ref, bs_ref, wout_ref, g_ref, b_ref,
                 out_ref, *, tm, rt, width, alpha):
    gw = width // GMLP_GROUPS
    tri = (lax.broadcasted_iota(jnp.int32, (GMLP_CHUNK, GMLP_CHUNK), 0)
           >= lax.broadcasted_iota(jnp.int32, (GMLP_CHUNK, GMLP_CHUNK), 1))
    w_causal = [jnp.where(tri, ws_ref[g], 0.0).astype(_BF16) for g in range(GMLP_GROUPS)]

    def proj_in(i):
        xb = x_ref[0, i * rt:(i + 1) * rt, :].astype(_BF16)
        return _dot(xb, win_ref[:, :width]), _dot(xb, win_ref[:, width:])

    zz = proj_in(0)
    for i in range(tm // rt):
        nxt = proj_in(i + 1) if (i + 1) * rt < tm else None
        u = _gelu_tanh(zz[0])
        v = _gelu_tanh(zz[1])
        vn = _layer_norm(v, lng_ref[...], lnb_ref[...]).astype(_BF16)
        chunks = []
        for ch in range(rt // GMLP_CHUNK):
            r0 = ch * GMLP_CHUNK
            cols = []
            for g in range(GMLP_GROUPS):
                cols.append(_dot(w_causal[g], vn[r0:r0 + GMLP_CHUNK, g * gw:(g + 1) * gw]))
            chunks.append(jnp.concatenate(cols, axis=1) + bs_ref[...])
        s = jnp.concatenate(chunks, axis=0)
        m = _dot((u * s).astype(_BF16), wout_ref[...])
        rows = slice(i * rt, (i + 1) * rt)
        out_ref[0, rows, :] = _layer_norm(alpha * x_ref[0, rows, :] + m, g_ref[...], b_ref[...])
        zz = nxt


def _gmlp_mixer(h, w_in, ln_g, ln_b, w_s, bs_full, w_out, g, b, alpha):
    B, S, D = h.shape
    width = w_out.shape[0]
    tm = min(GMLP_ROWS, S)
    rt = min(GMLP_SUBTILE, tm)
    const2 = lambda b, s: (0, 0)
    return pl.pallas_call(
        functools.partial(_gmlp_kernel, tm=tm, rt=rt, width=width, alpha=alpha),
        grid=(B, S // tm),
        in_specs=[
            pl.BlockSpec((1, tm, D), lambda b, s: (b, s, 0)),
            _resident((D, 2 * width), const2),
            _resident((1, width), const2),
            _resident((1, width), const2),
            _resident((GMLP_GROUPS, GMLP_CHUNK, GMLP_CHUNK), lambda b, s: (0, 0, 0)),
            _resident((GMLP_CHUNK, width), const2),
            _resident((width, D), const2),
            _resident((1, D), const2),
            _resident((1, D), const2),
        ],
        out_specs=pl.BlockSpec((1, tm, D), lambda b, s: (b, s, 0)),
        out_shape=jax.ShapeDtypeStruct((B, S, D), _F32),
        compiler_params=pltpu.CompilerParams(
            dimension_semantics=("parallel", "parallel"), vmem_limit_bytes=VMEM_LIMIT_BYTES),
        name="gmlp_mixer",
    )(h, w_in, ln_g, ln_b, w_s, bs_full, w_out, g, b)


PACKED_ROWS = 2 * SUBLANES


def _slab_tokens(v, zero):
    u = pltpu.bitcast(v, jnp.uint32)
    tokens = []
    for j in range(u.shape[0] // SUBLANES):
        for l in range(u.shape[1] // LANES):
            tokens.append(u[SUBLANES * j:SUBLANES * (j + 1), LANES * l:LANES * (l + 1)] & zero)
    return tokens


def _tie_to_tokens(x, tokens, k_tile):
    n_slabs = x.shape[0] // PACKED_ROWS
    n_lanes = x.shape[1] // LANES
    pieces = [[x[PACKED_ROWS * i:PACKED_ROWS * (i + 1), LANES * l:LANES * (l + 1)] for l in range(n_lanes)]
              for i in range(n_slabs)]
    n_pos = (x.shape[1] // k_tile) * n_slabs
    for j, tok in enumerate(tokens):
        p = j * n_pos // len(tokens)
        k, i = p // n_slabs, p % n_slabs
        l = k * (k_tile // LANES)
        pieces[i][l] = pieces[i][l] + pltpu.bitcast(tok, _BF16)
    return jnp.concatenate([jnp.concatenate(row, axis=1) for row in pieces], axis=0)


def _ffn_kernel(x_ref, wup_ref, cw_ref, cb_ref, wdown_ref, g_ref, b_ref, zero_ref, out_ref,
                acc_ref, tail_ref, xb_ref, a0_ref, a1_ref, a2_ref, a3_ref,
                gated0_ref, gated1_ref, gated2_ref, gated3_ref, *, tm, rs, n_chunks, alpha):
    a_refs = (a0_ref, a1_ref, a2_ref, a3_ref)
    gated_refs = (gated0_ref, gated1_ref, gated2_ref, gated3_ref)
    n_slabs = tm // rs
    n_items = n_chunks * n_slabs

    @pl.when(pl.program_id(1) == 0)
    def _():
        tail_ref[...] = jnp.zeros_like(tail_ref)

    def cast_slab(r):
        xb_ref[r * rs:(r + 1) * rs, :] = x_ref[0, r * rs:(r + 1) * rs, :].astype(_BF16)

    def item(t):
        return t // n_slabs, pl.multiple_of((t % n_slabs) * rs, rs)

    def up(t, slot, tokens=()):
        c, r0 = item(t)
        xs = xb_ref[pl.ds(r0, rs), :]
        if tokens:
            xs = _tie_to_tokens(xs, tokens, MXU_TILE)
        for half in range(2):
            a_refs[slot][half] = _dot(xs, wup_ref[half * n_chunks + c])

    def conv_gate(t, slot_a, slot_g):
        c, _ = item(t)

        def conv_half(half):
            idx = half * n_chunks + c
            a = a_refs[slot_a][half]
            prev = tail_ref[idx]
            tail_ref[idx] = a[rs - SUBLANES:, :]
            ext = jnp.concatenate([prev, a], axis=0)
            a1 = pltpu.roll(ext, 1, axis=0)[SUBLANES:, :]
            a2 = pltpu.roll(ext, 2, axis=0)[SUBLANES:, :]
            w = cw_ref[idx]
            return cb_ref[idx] + w[0:1, :] * a2 + w[1:2, :] * a1 + w[2:3, :] * a

        gate = conv_half(0)
        val = conv_half(1)
        gated = (gate * jax.nn.sigmoid(gate) * val).astype(_BF16)
        gated_refs[slot_g][...] = gated
        return _slab_tokens(gated, zero_ref[...])

    def down(t, slot):
        c, r0 = item(t)
        acc_ref[pl.ds(r0, rs), :] += _dot(gated_refs[slot][...], wdown_ref[c])

    def stage(t, k, do_down=True, do_up=True):
        tokens = conv_gate(t, k % 4, k % 4)
        if do_up:
            up(t + 2, (k + 2) % 4, tokens)
        if do_down:
            down(t - 1, (k - 1) % 4)

    cast_slab(0)
    up(0, 0)
    if n_slabs > 1:
        cast_slab(1)
    up(1, 1)
    for r in range(2, n_slabs):
        cast_slab(r)
    acc_ref[...] = jnp.zeros_like(acc_ref)
    stage(0, 0, do_down=False)
    stage(1, 1)

    def group(q, carry):
        t = FFN_UNROLL * q + 2
        for k in range(FFN_UNROLL):
            stage(t + k, (2 + k) % 4)
        return carry

    lax.fori_loop(0, (n_items - 4) // FFN_UNROLL, group, 0)
    stage(n_items - 2, (n_items - 2) % 4, do_up=False)
    stage(n_items - 1, (n_items - 1) % 4, do_up=False)
    down(n_items - 1, (n_items - 1) % 4)
    out_ref[0] = _layer_norm(alpha * x_ref[0] + acc_ref[...], g_ref[...], b_ref[...])


def _conv_ffn(h, wup_c, cw_c, cb_c, wdown_c, g, b, alpha):
    B, S, D = h.shape
    n2, _, fc = wup_c.shape
    n_chunks = n2 // 2
    tm = min(FFN_ROWS, S)
    rs = min(FFN_SLAB, tm)
    n_items = n_chunks * (tm // rs)
    assert n_items >= 4 + FFN_UNROLL and (n_items - 4) % FFN_UNROLL == 0
    const2 = lambda b, s: (0, 0)
    const3 = lambda b, s: (0, 0, 0)
    return pl.pallas_call(
        functools.partial(_ffn_kernel, tm=tm, rs=rs, n_chunks=n_chunks, alpha=alpha),
        grid=(B, S // tm),
        in_specs=[
            pl.BlockSpec((1, tm, D), lambda b, s: (b, s, 0)),
            _resident((n2, D, fc), const3),
            _resident((n2, CONV_WIDTH, fc), const3),
            _resident((n2, 1, fc), const3),
            _resident((n_chunks, fc, D), const3),
            _resident((1, D), const2),
            _resident((1, D), const2),
            _resident((SUBLANES, LANES), const2),
        ],
        out_specs=pl.BlockSpec((1, tm, D), lambda b, s: (b, s, 0)),
        out_shape=jax.ShapeDtypeStruct((B, S, D), _F32),
        scratch_shapes=[
            pltpu.VMEM((tm, D), _F32),
            pltpu.VMEM((n2, SUBLANES, fc), _F32),
            pltpu.VMEM((tm, D), _BF16),
            pltpu.VMEM((2, rs, fc), _F32),
            pltpu.VMEM((2, rs, fc), _F32),
            pltpu.VMEM((2, rs, fc), _F32),
            pltpu.VMEM((2, rs, fc), _F32),
            pltpu.VMEM((rs, fc), _BF16),
            pltpu.VMEM((rs, fc), _BF16),
            pltpu.VMEM((rs, fc), _BF16),
            pltpu.VMEM((rs, fc), _BF16),
        ],
        compiler_params=pltpu.CompilerParams(
            dimension_semantics=("parallel", "arbitrary"), vmem_limit_bytes=VMEM_LIMIT_BYTES),
        name="conv_ffn",
    )(h, wup_c, cw_c, cb_c, wdown_c, g, b, jnp.zeros((SUBLANES, LANES), jnp.uint32))


QKV_ROWS = 1024
QKV_SUBTILE = 256
MIX_ROWS = 2048
MIX_SUBTILE = 256
GMLP_ROWS = 512
GMLP_SUBTILE = 256
FFN_ROWS = 1024
FFN_SLAB = 256
FFN_UNROLL = 20
FFN_CHUNK = 256


def _chunk_cols(w, fc):
    r, c = w.shape
    return jnp.transpose(w.reshape(r, c // fc, fc), (1, 0, 2))


def kernel(x, attn_w_in, attn_w_out, gmlp_w_in, gmlp_ln_g, gmlp_ln_b, gmlp_w_s, gmlp_b_s, gmlp_w_out,
           ffn_w_up, ffn_conv_w, ffn_conv_b, ffn_w_down, ln_mix_g, ln_mix_b, ln_ffn_g, ln_ffn_b):
    B, S, D = x.shape
    depth = ffn_w_up.shape[0]
    alpha = (2 * depth) ** 0.25
    d_ff = ffn_w_down.shape[1]
    width = gmlp_w_out.shape[1]
    row = lambda v: v.reshape(1, -1)

    h = x
    for i in range(depth):
        j = i // 2
        if i % 2 == 0:
            w_in = attn_w_in[j]
            wq = w_in[:, :D].astype(_BF16)
            wk = w_in[:, D:2 * D].astype(_BF16)
            wvt = w_in[:, 2 * D:].T.astype(_BF16)
            q, kp, vt = _qkv_proj(h, wq, wk, wvt)
            o = _sb_attention(q, kp, vt)
            h = _mix_out_ln(o, h, attn_w_out[j].astype(_BF16), row(ln_mix_g[i]), row(ln_mix_b[i]), alpha)
        else:
            bs_full = jnp.repeat(gmlp_b_s[j].T, width // GMLP_GROUPS, axis=1)
            h = _gmlp_mixer(h, gmlp_w_in[j].astype(_BF16), row(gmlp_ln_g[j]), row(gmlp_ln_b[j]),
                            gmlp_w_s[j], bs_full, gmlp_w_out[j].astype(_BF16),
                            row(ln_mix_g[i]), row(ln_mix_b[i]), alpha)
        wup_c = _chunk_cols(ffn_w_up[i].astype(_BF16), FFN_CHUNK)
        cw_c = _chunk_cols(ffn_conv_w[i], FFN_CHUNK)
        cb_c = _chunk_cols(ffn_conv_b[i].reshape(1, -1), FFN_CHUNK)
        wdown_c = ffn_w_down[i].astype(_BF16).reshape(d_ff // FFN_CHUNK, FFN_CHUNK, D)
        h = _conv_ffn(h, wup_c, cw_c, cb_c, wdown_c, row(ln_ffn_g[i]), row(ln_ffn_b[i]), alpha)
    return h
```

```python
import functools
import math

import jax
import jax.numpy as jnp
from jax import lax
from jax.experimental import pallas as pl
from jax.experimental.pallas import tpu as pltpu

LN_EPS = 1e-5
CONV_WIDTH = 3
SB_HEAD_DIM = 64
GMLP_GROUPS = 8
GMLP_CHUNK = 128

SUBLANES = 8
LANES = 128
MXU_TILE = 256
KEY_BLOCK = 128
KEY_GROUP = KEY_BLOCK // SUBLANES
VMEM_LIMIT_BYTES = 56 * 1024 * 1024

ATT_TQ = 256
ATT_KB = 256
ATT_LANES = 1024
ATT_CHAIN = 4
LOG2E = 1.0 / math.log(2.0)
INV_LN2 = 1.0 / math.log(2.0)
LOG2_DEAD = 150.0

_BF16 = jnp.bfloat16
_F32 = jnp.float32


def _resident(block_shape, index_map):
    return pl.BlockSpec(block_shape, index_map, pipeline_mode=pl.Buffered(1))


def _layer_norm(y, g, b):
    mu = jnp.mean(y, axis=-1, keepdims=True)
    d = y - mu
    var = jnp.mean(d * d, axis=-1, keepdims=True)
    return d * lax.rsqrt(var + LN_EPS) * g + b


def _dot(a, b):
    return jnp.dot(a, b, preferred_element_type=_F32)


def _dot_nt(a, b):
    return lax.dot_general(a, b, (((1,), (1,)), ((), ())), preferred_element_type=_F32)


def _qkv_kernel(h_ref, wq_ref, wk_ref, wvt_ref, q_ref, kp_ref, vt_ref, *, tm, rt, scale):
    row = lax.broadcasted_iota(jnp.int32, (KEY_BLOCK, KEY_BLOCK), 0)
    col = lax.broadcasted_iota(jnp.int32, (KEY_BLOCK, KEY_BLOCK), 1)
    perm = (col == (row % SUBLANES) * KEY_GROUP + row // SUBLANES).astype(_BF16)

    def project(i):
        hb = h_ref[0, i * rt:(i + 1) * rt, :].astype(_BF16)
        hp = jnp.concatenate(
            [_dot(perm, hb[blk * KEY_BLOCK:(blk + 1) * KEY_BLOCK, :]) for blk in range(rt // KEY_BLOCK)],
            axis=0).astype(_BF16)
        return _dot(hb, wq_ref[...]), _dot(hp, wk_ref[...]), _dot_nt(wvt_ref[...], hp)

    cur = project(0)
    for i in range(tm // rt):
        nxt = project(i + 1) if (i + 1) * rt < tm else None
        rows = slice(i * rt, (i + 1) * rt)
        q_ref[0, rows, :] = (cur[0] * scale).astype(_BF16)
        kp_ref[0, rows, :] = cur[1].astype(_BF16)
        vt_ref[0, :, rows] = cur[2].astype(_BF16)
        cur = nxt


def _qkv_proj(h, wq, wk, wvt):
    B, S, D = h.shape
    tm = min(QKV_ROWS, S)
    rt = min(QKV_SUBTILE, tm)
    scale = SB_HEAD_DIM ** -0.5 * LOG2E
    return pl.pallas_call(
        functools.partial(_qkv_kernel, tm=tm, rt=rt, scale=scale),
        grid=(B, S // tm),
        in_specs=[
            pl.BlockSpec((1, tm, D), lambda b, s: (b, s, 0)),
            _resident((D, D), lambda b, s: (0, 0)),
            _resident((D, D), lambda b, s: (0, 0)),
            _resident((D, D), lambda b, s: (0, 0)),
        ],
        out_specs=[
            pl.BlockSpec((1, tm, D), lambda b, s: (b, s, 0)),
            pl.BlockSpec((1, tm, D), lambda b, s: (b, s, 0)),
            pl.BlockSpec((1, D, tm), lambda b, s: (b, 0, s)),
        ],
        out_shape=[
            jax.ShapeDtypeStruct((B, S, D), _BF16),
            jax.ShapeDtypeStruct((B, S, D), _BF16),
            jax.ShapeDtypeStruct((B, D, S), _BF16),
        ],
        compiler_params=pltpu.CompilerParams(
            dimension_semantics=("parallel", "parallel"), vmem_limit_bytes=VMEM_LIMIT_BYTES),
        name="qkv_proj",
    )(h, wq, wk, wvt)


def _sublane_suffix_scan(g):
    row = lax.broadcasted_iota(jnp.int32, g.shape, 0)
    x = g
    for sh in (1, 2, 4):
        shifted = pltpu.roll(x, SUBLANES - sh, axis=0)
        x = x + jnp.where(row + sh < SUBLANES, shifted, 0.0)
    return x


def _sb_block(z, carry, causal):
    m = jnp.maximum(z, 0.0) + jnp.log(1.0 + jnp.exp2(-jnp.abs(z))) * INV_LN2
    if causal is not None:
        m = jnp.where(causal, m, 0.0)
    tiles = [m[SUBLANES * v:SUBLANES * (v + 1), :] for v in range(KEY_GROUP)]
    run = [None] * KEY_GROUP
    run[KEY_GROUP - 1] = tiles[KEY_GROUP - 1]
    for v in range(KEY_GROUP - 2, -1, -1):
        run[v] = run[v + 1] + tiles[v]
    group_tot = run[0]
    incl = _sublane_suffix_scan(group_tot)
    base = (incl - group_tot) + carry
    total = jnp.concatenate([run[v] + base for v in range(KEY_GROUP)], axis=0)
    a = jnp.exp2(z - total)
    if causal is not None:
        a = jnp.where(causal, a, 0.0)
    new_carry = carry + jnp.broadcast_to(incl[0:1, :], carry.shape)
    return a.astype(_BF16), new_carry


def _attn_kernel(q_ref, kp_ref, vt_ref, o_ref, acc_ref, carry_ref, *, tq, n_heads):
    qi = pl.program_id(2)
    lane = lax.broadcasted_iota(jnp.int32, (tq, LANES), 1)
    qm = []
    for h in range(n_heads):
        qpair = q_ref[0, :, LANES * (h // 2):LANES * (h // 2 + 1)]
        qm.append(jnp.where((lane // SB_HEAD_DIM) == h % 2, qpair, jnp.zeros_like(qpair)))

    acc_ref[...] = jnp.zeros_like(acc_ref)
    carry_ref[...] = jnp.zeros_like(carry_ref)

    def causal_mask(sub, q0):
        row = lax.broadcasted_iota(jnp.int32, (KEY_BLOCK, tq - q0), 0)
        key_off = (row % SUBLANES) * KEY_GROUP + row // SUBLANES + sub * KEY_BLOCK
        return key_off < lax.broadcasted_iota(jnp.int32, (KEY_BLOCK, tq - q0), 1) + q0

    def key_step(j, diagonal):
        n_sub = ATT_KB // KEY_BLOCK
        first_q = [sub * KEY_BLOCK if diagonal else 0 for sub in range(n_sub)]

        def scores(h):
            out = [None] * n_sub
            for sub in range(n_sub - 1, -1, -1):
                start = pl.multiple_of(j * ATT_KB + sub * KEY_BLOCK, KEY_BLOCK)
                kblk = kp_ref[0, pl.ds(start, KEY_BLOCK), LANES * (h // 2):LANES * (h // 2 + 1)]
                out[sub] = _dot_nt(kblk, qm[h][first_q[sub]:, :])
            return out

        z = {h: scores(h) for h in range(min(ATT_CHAIN, n_heads))}
        for h in range(n_heads):
            a_parts = [None] * n_sub
            for sub in range(n_sub - 1, -1, -1):
                q0 = first_q[sub]
                mask = causal_mask(sub, q0) if diagonal else None
                carry = carry_ref[h]
                a, new_carry = _sb_block(z[h][sub], carry[:, q0:], mask)
                if q0:
                    a = jnp.concatenate([jnp.zeros((KEY_BLOCK, q0), a.dtype), a], axis=1)
                    new_carry = jnp.concatenate([carry[:, :q0], new_carry], axis=1)
                a_parts[sub], carry_ref[h] = a, new_carry
            if h + ATT_CHAIN < n_heads:
                z[h + ATT_CHAIN] = scores(h + ATT_CHAIN)
            vt = vt_ref[0, SB_HEAD_DIM * h:SB_HEAD_DIM * (h + 1),
                        pl.ds(pl.multiple_of(j * ATT_KB, ATT_KB), ATT_KB)]
            acc_ref[SB_HEAD_DIM * h:SB_HEAD_DIM * (h + 1), :] += _dot(vt, jnp.concatenate(a_parts, axis=0))

    def any_alive():
        return jnp.min(carry_ref[...]) < LOG2_DEAD

    key_step(qi, True)

    def body(state):
        j, _ = state
        key_step(j, False)
        return j - 1, any_alive()

    lax.while_loop(lambda st: jnp.logical_and(st[0] >= 0, st[1]), body, (qi - 1, any_alive()))
    o_ref[0] = acc_ref[...].T.astype(o_ref.dtype)


def _sb_attention(q, kp, vt):
    B, S, D = q.shape
    tq = min(ATT_TQ, S)
    assert tq == ATT_KB and S % tq == 0 and D % ATT_LANES == 0
    n_heads = ATT_LANES // SB_HEAD_DIM
    return pl.pallas_call(
        functools.partial(_attn_kernel, tq=tq, n_heads=n_heads),
        grid=(B, D // ATT_LANES, S // tq),
        in_specs=[
            pl.BlockSpec((1, tq, ATT_LANES), lambda b, h, i: (b, i, h)),
            pl.BlockSpec((1, S, ATT_LANES), lambda b, h, i: (b, 0, h)),
            pl.BlockSpec((1, ATT_LANES, S), lambda b, h, i: (b, h, 0)),
        ],
        out_specs=pl.BlockSpec((1, tq, ATT_LANES), lambda b, h, i: (b, i, h)),
        out_shape=jax.ShapeDtypeStruct((B, S, D), _BF16),
        scratch_shapes=[
            pltpu.VMEM((ATT_LANES, tq), _F32),
            pltpu.VMEM((n_heads, SUBLANES, tq), _F32),
        ],
        compiler_params=pltpu.CompilerParams(
            dimension_semantics=("parallel", "parallel", "parallel"),
            vmem_limit_bytes=VMEM_LIMIT_BYTES),
        name="sb_attention",
    )(q, kp, vt)


def _mix_out_kernel(o_ref, h_ref, w_ref, g_ref, b_ref, out_ref, *, tm, rt, alpha):
    def proj(i):
        return _dot(o_ref[0, i * rt:(i + 1) * rt, :], w_ref[...])

    m = proj(0)
    for i in range(tm // rt):
        nxt = proj(i + 1) if (i + 1) * rt < tm else None
        rows = slice(i * rt, (i + 1) * rt)
        out_ref[0, rows, :] = _layer_norm(alpha * h_ref[0, rows, :] + m, g_ref[...], b_ref[...])
        m = nxt


def _mix_out_ln(o, h, w_out, g, b, alpha):
    B, S, D = h.shape
    tm = min(MIX_ROWS, S)
    rt = min(MIX_SUBTILE, tm)
    return pl.pallas_call(
        functools.partial(_mix_out_kernel, tm=tm, rt=rt, alpha=alpha),
        grid=(B, S // tm),
        in_specs=[
            pl.BlockSpec((1, tm, D), lambda b, s: (b, s, 0)),
            pl.BlockSpec((1, tm, D), lambda b, s: (b, s, 0)),
            _resident((D, D), lambda b, s: (0, 0)),
            _resident((1, D), lambda b, s: (0, 0)),
            _resident((1, D), lambda b, s: (0, 0)),
        ],
        out_specs=pl.BlockSpec((1, tm, D), lambda b, s: (b, s, 0)),
        out_shape=jax.ShapeDtypeStruct((B, S, D), _F32),
        compiler_params=pltpu.CompilerParams(
            dimension_semantics=("parallel", "parallel"), vmem_limit_bytes=VMEM_LIMIT_BYTES),
        name="mix_out_ln",
    )(o, h, w_out, g, b)


def _gelu_tanh(x):
    c = math.sqrt(2.0 / math.pi)
    return 0.5 * x * (1.0 + jnp.tanh(c * (x + 0.044715 * (x * x * x))))


def _gmlp_kernel(x_ref, win_ref, lng_ref, lnb_ref, ws_ref, bs_ref, wout_ref, g_ref, b_ref,
                 out_ref, *, tm, rt, width, alpha):
    gw = width // GMLP_GROUPS
    tri = (lax.broadcasted_iota(jnp.int32, (GMLP_CHUNK, GMLP_CHUNK), 0)
           >= lax.broadcasted_iota(jnp.int32, (GMLP_CHUNK, GMLP_CHUNK), 1))
    w_causal = [jnp.where(tri, ws_ref[g], 0.0).astype(_BF16) for g in range(GMLP_GROUPS)]

    def proj_in(i):
        xb = x_ref[0, i * rt:(i + 1) * rt, :].astype(_BF16)
        return _dot(xb, win_ref[:, :width]), _dot(xb, win_ref[:, width:])

    zz = proj_in(0)
    for i in range(tm // rt):
        nxt = proj_in(i + 1) if (i + 1) * rt < tm else None
        u = _gelu_tanh(zz[0])
        v = _gelu_tanh(zz[1])
        vn = _layer_norm(v, lng_ref[...], lnb_ref[...]).astype(_BF16)
        chunks = []
        for ch in range(rt // GMLP_CHUNK):
            r0 = ch * GMLP_CHUNK
            cols = []
            for g in range(GMLP_GROUPS):
                cols.append(_dot(w_causal[g], vn[r0:r0 + GMLP_CHUNK, g * gw:(g + 1) * gw]))
            chunks.append(jnp.concatenate(cols, axis=1) + bs_ref[...])
        s = jnp.concatenate(chunks, axis=0)
        m = _dot((u * s).astype(_BF16), wout_ref[...])
        rows = slice(i * rt, (i + 1) * rt)
        out_ref[0, rows, :] = _layer_norm(alpha * x_ref[0, rows, :] + m, g_ref[...], b_ref[...])
        zz = nxt


def _gmlp_mixer(h, w_in, ln_g, ln_b, w_s, bs_full, w_out, g, b, alpha):
    B, S, D = h.shape
    width = w_out.shape[0]
    tm = min(GMLP_ROWS, S)
    rt = min(GMLP_SUBTILE, tm)
    const2 = lambda b, s: (0, 0)
    return pl.pallas_call(
        functools.partial(_gmlp_kernel, tm=tm, rt=rt, width=width, alpha=alpha),
        grid=(B, S // tm),
        in_specs=[
            pl.BlockSpec((1, tm, D), lambda b, s: (b, s, 0)),
            _resident((D, 2 * width), const2),
            _resident((1, width), const2),
            _resident((1, width), const2),
            _resident((GMLP_GROUPS, GMLP_CHUNK, GMLP_CHUNK), lambda b, s: (0, 0, 0)),
            _resident((GMLP_CHUNK, width), const2),
            _resident((width, D), const2),
            _resident((1, D), const2),
            _resident((1, D), const2),
        ],
        out_specs=pl.BlockSpec((1, tm, D), lambda b, s: (b, s, 0)),
        out_shape=jax.ShapeDtypeStruct((B, S, D), _F32),
        compiler_params=pltpu.CompilerParams(
            dimension_semantics=("parallel", "parallel"), vmem_limit_bytes=VMEM_LIMIT_BYTES),
        name="gmlp_mixer",
    )(h, w_in, ln_g, ln_b, w_s, bs_full, w_out, g, b)


PACKED_ROWS = 2 * SUBLANES


def _slab_tokens(v, zero):
    u = pltpu.bitcast(v, jnp.uint32)
    tokens = []
    for j in range(u.shape[0] // SUBLANES):
        for l in range(u.shape[1] // LANES):
            tokens.append(u[SUBLANES * j:SUBLANES * (j + 1), LANES * l:LANES * (l + 1)] & zero)
    return tokens


def _tie_to_tokens(x, tokens, k_tile):
    n_slabs = x.shape[0] // PACKED_ROWS
    n_lanes = x.shape[1] // LANES
    pieces = [[x[PACKED_ROWS * i:PACKED_ROWS * (i + 1), LANES * l:LANES * (l + 1)] for l in range(n_lanes)]
              for i in range(n_slabs)]
    n_pos = (x.shape[1] // k_tile) * n_slabs
    for j, tok in enumerate(tokens):
        p = j * n_pos // len(tokens)
        k, i = p // n_slabs, p % n_slabs
        l = k * (k_tile // LANES)
        pieces[i][l] = pieces[i][l] + pltpu.bitcast(tok, _BF16)
    return jnp.concatenate([jnp.concatenate(row, axis=1) for row in pieces], axis=0)


def _ffn_kernel(x_ref, wup_ref, cw_ref, cb_ref, wdown_ref, g_ref, b_ref, zero_ref, out_ref,
                acc_ref, tail_ref, xb_ref, a0_ref, a1_ref, a2_ref, a3_ref,
                gated0_ref, gated1_ref, gated2_ref, gated3_ref, *, tm, rs, n_chunks, alpha):
    a_refs = (a0_ref, a1_ref, a2_ref, a3_ref)
    gated_refs = (gated0_ref, gated1_ref, gated2_ref, gated3_ref)
    n_slabs = tm // rs
    n_items = n_chunks * n_slabs

    @pl.when(pl.program_id(1) == 0)
    def _():
        tail_ref[...] = jnp.zeros_like(tail_ref)

    def cast_slab(r):
        xb_ref[r * rs:(r + 1) * rs, :] = x_ref[0, r * rs:(r + 1) * rs, :].astype(_BF16)

    def item(t):
        return t // n_slabs, pl.multiple_of((t % n_slabs) * rs, rs)

    def up(t, slot, tokens=()):
        c, r0 = item(t)
        xs = xb_ref[pl.ds(r0, rs), :]
        if tokens:
            xs = _tie_to_tokens(xs, tokens, MXU_TILE)
        for half in range(2):
            a_refs[slot][half] = _dot(xs, wup_ref[half * n_chunks + c])

    def conv_gate(t, slot_a, slot_g):
        c, _ = item(t)

        def conv_half(half):
            idx = half * n_chunks + c
            a = a_refs[slot_a][half]
            prev = tail_ref[idx]
            tail_ref[idx] = a[rs - SUBLANES:, :]
            ext = jnp.concatenate([prev, a], axis=0)
            a1 = pltpu.roll(ext, 1, axis=0)[SUBLANES:, :]
            a2 = pltpu.roll(ext, 2, axis=0)[SUBLANES:, :]
            w = cw_ref[idx]
            return cb_ref[idx] + w[0:1, :] * a2 + w[1:2, :] * a1 + w[2:3, :] * a

        gate = conv_half(0)
        val = conv_half(1)
        gated = (gate * jax.nn.sigmoid(gate) * val).astype(_BF16)
        gated_refs[slot_g][...] = gated
        return _slab_tokens(gated, zero_ref[...])

    def down(t, slot):
        c, r0 = item(t)
        acc_ref[pl.ds(r0, rs), :] += _dot(gated_refs[slot][...], wdown_ref[c])

    def stage(t, k, do_down=True, do_up=True):
        tokens = conv_gate(t, k % 4, k % 4)
        if do_up:
            up(t + 2, (k + 2) % 4, tokens)
        if do_down:
            down(t - 1, (k - 1) % 4)

    cast_slab(0)
    up(0, 0)
    if n_slabs > 1:
        cast_slab(1)
    up(1, 1)
    for r in range(2, n_slabs):
        cast_slab(r)
    acc_ref[...] = jnp.zeros_like(acc_ref)
    stage(0, 0, do_down=False)
    stage(1, 1)

    def group(q, carry):
        t = FFN_UNROLL * q + 2
        for k in range(FFN_UNROLL):
            stage(t + k, (2 + k) % 4)
        return carry

    lax.fori_loop(0, (n_items - 4) // FFN_UNROLL, group, 0)
    stage(n_items - 2, (n_items - 2) % 4, do_up=False)
    stage(n_items - 1, (n_items - 1) % 4, do_up=False)
    down(n_items - 1, (n_items - 1) % 4)
    out_ref[0] = _layer_norm(alpha * x_ref[0] + acc_ref[...], g_ref[...], b_ref[...])


def _conv_ffn(h, wup_c, cw_c, cb_c, wdown_c, g, b, alpha):
    B, S, D = h.shape
    n2, _, fc = wup_c.shape
    n_chunks = n2 // 2
    tm = min(FFN_ROWS, S)
    rs = min(FFN_SLAB, tm)
    n_items = n_chunks * (tm // rs)
    assert n_items >= 4 + FFN_UNROLL and (n_items - 4) % FFN_UNROLL == 0
    const2 = lambda b, s: (0, 0)
    const3 = lambda b, s: (0, 0, 0)
    return pl.pallas_call(
        functools.partial(_ffn_kernel, tm=tm, rs=rs, n_chunks=n_chunks, alpha=alpha),
        grid=(B, S // tm),
        in_specs=[
            pl.BlockSpec((1, tm, D), lambda b, s: (b, s, 0)),
            _resident((n2, D, fc), const3),
            _resident((n2, CONV_WIDTH, fc), const3),
            _resident((n2, 1, fc), const3),
            _resident((n_chunks, fc, D), const3),
            _resident((1, D), const2),
            _resident((1, D), const2),
            _resident((SUBLANES, LANES), const2),
        ],
        out_specs=pl.BlockSpec((1, tm, D), lambda b, s: (b, s, 0)),
        out_shape=jax.ShapeDtypeStruct((B, S, D), _F32),
        scratch_shapes=[
            pltpu.VMEM((tm, D), _F32),
            pltpu.VMEM((n2, SUBLANES, fc), _F32),
            pltpu.VMEM((tm, D), _BF16),
            pltpu.VMEM((2, rs, fc), _F32),
            pltpu.VMEM((2, rs, fc), _F32),
            pltpu.VMEM((2, rs, fc), _F32),
            pltpu.VMEM((2, rs, fc), _F32),
            pltpu.VMEM((rs, fc), _BF16),
            pltpu.VMEM((rs, fc), _BF16),
            pltpu.VMEM((rs, fc), _BF16),
            pltpu.VMEM((rs, fc), _BF16),
        ],
        compiler_params=pltpu.CompilerParams(
            dimension_semantics=("parallel", "arbitrary"), vmem_limit_bytes=VMEM_LIMIT_BYTES),
        name="conv_ffn",
    )(h, wup_c, cw_c, cb_c, wdown_c, g, b, jnp.zeros((SUBLANES, LANES), jnp.uint32))


QKV_ROWS = 2048
QKV_SUBTILE = 256
MIX_ROWS = 2048
MIX_SUBTILE = 256
GMLP_ROWS = 512
GMLP_SUBTILE = 256
FFN_ROWS = 1024
FFN_SLAB = 256
FFN_UNROLL = 20
FFN_CHUNK = 256


def _chunk_cols(w, fc):
    r, c = w.shape
    return jnp.transpose(w.reshape(r, c // fc, fc), (1, 0, 2))


def kernel(x, attn_w_in, attn_w_out, gmlp_w_in, gmlp_ln_g, gmlp_ln_b, gmlp_w_s, gmlp_b_s, gmlp_w_out,
           ffn_w_up, ffn_conv_w, ffn_conv_b, ffn_w_down, ln_mix_g, ln_mix_b, ln_ffn_g, ln_ffn_b):
    B, S, D = x.shape
    depth = ffn_w_up.shape[0]
    alpha = (2 * depth) ** 0.25
    d_ff = ffn_w_down.shape[1]
    width = gmlp_w_out.shape[1]
    row = lambda v: v.reshape(1, -1)

    h = x
    for i in range(depth):
        j = i // 2
        if i % 2 == 0:
            w_in = attn_w_in[j]
            wq = w_in[:, :D].astype(_BF16)
            wk = w_in[:, D:2 * D].astype(_BF16)
            wvt = w_in[:, 2 * D:].T.astype(_BF16)
            q, kp, vt = _qkv_proj(h, wq, wk, wvt)
            o = _sb_attention(q, kp, vt)
            h = _mix_out_ln(o, h, attn_w_out[j].astype(_BF16), row(ln_mix_g[i]), row(ln_mix_b[i]), alpha)
        else:
            bs_full = jnp.repeat(gmlp_b_s[j].T, width // GMLP_GROUPS, axis=1)
            h = _gmlp_mixer(h, gmlp_w_in[j].astype(_BF16), row(gmlp_ln_g[j]), row(gmlp_ln_b[j]),
                            gmlp_w_s[j], bs_full, gmlp_w_out[j].astype(_BF16),
                            row(ln_mix_g[i]), row(ln_mix_b[i]), alpha)
        wup_c = _chunk_cols(ffn_w_up[i].astype(_BF16), FFN_CHUNK)
        cw_c = _chunk_cols(ffn_conv_w[i], FFN_CHUNK)
        cb_c = _chunk_cols(ffn_conv_b[i].reshape(1, -1), FFN_CHUNK)
        wdown_c = ffn_w_down[i].astype(_BF16).reshape(d_ff // FFN_CHUNK, FFN_CHUNK, D)
        h = _conv_ffn(h, wup_c, cw_c, cb_c, wdown_c, row(ln_ffn_g[i]), row(ln_ffn_b[i]), alpha)
    return h
```
